```python
import jax, jax.numpy as jnp
from jax import lax
import numpy as np

D_MODEL = 1024
BATCH = 32
SEQ = 256
DEPTH = 2
DEC_BATCH = 4
DEC_SEQ = 2048
PAST_LEN = 256

GRID_W = 64
MLA_HEADS = 8
Q_LORA = 256
KV_LORA = 128
QK_NOPE = 64
QK_ROPE = 32
V_HEAD = 64
QK_HEAD = QK_NOPE + QK_ROPE
MLA_W = MLA_HEADS * V_HEAD
CONV_W = 256
CONV_GROUPS = 4
RET_HEADS = 4
RET_DK = 64
RET_DV = 64
RET_W = RET_HEADS * RET_DV
RET_CHUNK = 128
MIX_W = MLA_W + CONV_W + RET_W
IN_COLS = Q_LORA + KV_LORA + QK_ROPE + 3 * CONV_W + 2 * RET_HEADS * RET_DK + 2 * RET_W
FFN_HIDDEN = ((8 * D_MODEL // 3 + 255) // 256) * 256
Q_BLOCK = 128
ROPE_THETA = 10000.0
EPS = 1e-6

kernel_name = "hybrid_mla_conv_retention_dit_step"


def _rmsnorm(x, g=None):
    x32 = x.astype(jnp.float32)
    y = (x32 * lax.rsqrt(jnp.mean(x32 * x32, axis=-1, keepdims=True) + EPS)).astype(x.dtype)
    return y if g is None else y * g


def _modulation(cond, ada_w, ada_b):
    m = jax.nn.silu(cond) @ ada_w + ada_b
    return [p[:, None, :] for p in jnp.split(m, 6, axis=-1)]


def _modulate(h, shift, scale):
    return h * (1.0 + scale) + shift


def _rope_2d(x):
    n = x.shape[1]
    rows = n // GRID_W
    row = jnp.repeat(jnp.arange(rows), GRID_W).astype(jnp.float32)
    col = jnp.tile(jnp.arange(GRID_W), rows).astype(jnp.float32)
    half = QK_ROPE // 2
    freqs = jnp.power(ROPE_THETA, -jnp.arange(0, half, 2, dtype=jnp.float32) / half)

    def rot(xa, pos):
        ang = pos[:, None] * freqs[None, :]
        cos = jnp.cos(ang)[None, :, None, :]
        sin = jnp.sin(ang)[None, :, None, :]
        x1, x2 = jnp.split(xa.astype(jnp.float32), 2, axis=-1)
        return jnp.concatenate([x1 * cos - x2 * sin, x1 * sin + x2 * cos], axis=-1)

    xr, xc = jnp.split(x, 2, axis=-1)
    return jnp.concatenate([rot(xr, row), rot(xc, col)], axis=-1).astype(x.dtype)


def _attention(q, k, v):
    b, h, tq, dq = q.shape
    nblk = tq // Q_BLOCK
    qb = jnp.moveaxis(q.reshape(b, h, nblk, Q_BLOCK, dq), 2, 0)
    scale = QK_HEAD ** -0.5

    def block(qi):
        s = jnp.einsum("bhqd,bhkd->bhqk", qi, k).astype(jnp.float32) * scale
        p = jax.nn.softmax(s, axis=-1).astype(v.dtype)
        return jnp.einsum("bhqk,bhkd->bhqd", p, v)

    o = lax.map(block, qb)
    return jnp.moveaxis(o, 0, 2).reshape(b, h, tq, V_HEAD)


def _mla_kv(c_kv, k_pe, w_kv_up, k_head_g):
    b, t, _ = c_kv.shape
    kv = (c_kv @ w_kv_up).reshape(b, t, MLA_HEADS, QK_NOPE + V_HEAD)
    k_nope, v = jnp.split(kv, [QK_NOPE], axis=-1)
    k = jnp.concatenate([k_nope, jnp.broadcast_to(k_pe[:, :, None, :], (b, t, MLA_HEADS, QK_ROPE))], axis=-1)
    return _rmsnorm(k, k_head_g), v


def _short_conv(gb, gc, xin, conv_w):
    u = gc * xin
    up = jnp.pad(u, ((0, 0), (1, 1), (0, 0)))
    y = up[:, :-2] * conv_w[0] + up[:, 1:-1] * conv_w[1] + up[:, 2:] * conv_w[2]
    return gb * y


def _retention_scan(q, k, v, log_gamma, s0):
    b, t, hh, _ = q.shape
    n = t // RET_CHUNK

    def chunks(a):
        a = a.astype(jnp.float32).reshape(b, n, RET_CHUNK, hh, a.shape[-1])
        return jnp.moveaxis(a, 1, 0).transpose(0, 1, 3, 2, 4)

    idx = jnp.arange(RET_CHUNK, dtype=jnp.float32)
    lg = log_gamma[:, None]
    diff = idx[:, None] - idx[None, :]
    decay = jnp.exp(jnp.where(diff >= 0, lg[:, :, None] * diff, -jnp.inf))
    q_dec = jnp.exp(lg * (idx + 1.0))[:, :, None]
    k_dec = jnp.exp(lg * (RET_CHUNK - 1.0 - idx))[:, :, None]
    c_dec = jnp.exp(lg * RET_CHUNK)[:, :, None]

    def step(s, inp):
        qi, ki, vi = inp
        scores = jnp.einsum("bhqd,bhkd->bhqk", qi, ki) * decay
        o = jnp.einsum("bhqk,bhkv->bhqv", scores, vi) + jnp.einsum("bhqd,bhdv->bhqv", qi * q_dec, s)
        s = s * c_dec + jnp.einsum("bhkd,bhkv->bhdv", ki * k_dec, vi)
        return s, o

    s_fin, o = lax.scan(step, s0.astype(jnp.float32), (chunks(q), chunks(k), chunks(v)))
    o = jnp.moveaxis(o.transpose(0, 1, 3, 2, 4), 0, 1).reshape(b, t, hh, RET_DV)
    return o, s_fin


def _retention(rq, rk, rv, rg, lg_f, lg_b, s0_f, s0_b):
    b, t, _ = rq.shape
    q = rq.reshape(b, t, RET_HEADS, RET_DK)
    k = rk.reshape(b, t, RET_HEADS, RET_DK) * (RET_DK ** -0.5)
    v = rv.reshape(b, t, RET_HEADS, RET_DV)
    o_f, s_f = _retention_scan(q, k, v, lg_f, s0_f)
    o_b, s_b = _retention_scan(q[:, ::-1], k[:, ::-1], v[:, ::-1], lg_b, s0_b)
    o = _rmsnorm(o_f + o_b[:, ::-1])
    return o.reshape(b, t, RET_W).astype(rg.dtype) * jax.nn.silu(rg), s_f, s_b


def _token_mixer(h, w_in, q_norm_g, kv_norm_g, w_q_up, w_kv_up, q_head_g, k_head_g,
                 conv_w, lg_f, lg_b, w_o, ctx_ckv=None, ctx_kpe=None, s0_f=None, s0_b=None):
    latent = ctx_ckv is not None
    b, t, _ = h.shape
    widths = [Q_LORA, KV_LORA, QK_ROPE, CONV_W, CONV_W, CONV_W,
              RET_HEADS * RET_DK, RET_HEADS * RET_DK, RET_W]
    splits = [int(s) for s in np.cumsum(widths)]
    q_lat, kv_lat, k_pe, gb, gc, xin, rq, rk, rv, rg = jnp.split(h @ w_in, splits, axis=-1)

    c_kv = _rmsnorm(kv_lat, kv_norm_g)
    q = (_rmsnorm(q_lat, q_norm_g) @ w_q_up).reshape(b, t, MLA_HEADS, QK_HEAD)
    q = _rmsnorm(q, q_head_g)
    k, v = _mla_kv(c_kv, k_pe, w_kv_up, k_head_g)
    if latent:
        q = jnp.concatenate([q[..., :QK_NOPE], _rope_2d(q[..., QK_NOPE:])], axis=-1)
        k = jnp.concatenate([k[..., :QK_NOPE], _rope_2d(k[..., QK_NOPE:])], axis=-1)
        k_c, v_c = _mla_kv(ctx_ckv, ctx_kpe, w_kv_up, k_head_g)
        k = jnp.concatenate([k, k_c], axis=1)
        v = jnp.concatenate([v, v_c], axis=1)
    else:
        s0_f = jnp.zeros((b, RET_HEADS, RET_DK, RET_DV), jnp.float32)
        s0_b = s0_f
    attn = _attention(q.transpose(0, 2, 1, 3), k.transpose(0, 2, 1, 3), v.transpose(0, 2, 1, 3))
    attn = attn.transpose(0, 2, 1, 3).reshape(b, t, MLA_W)

    conv = _short_conv(gb, gc, xin, conv_w)

    ret, s_f, s_b = _retention(rq, rk, rv, rg, lg_f, lg_b, s0_f, s0_b)

    out = jnp.concatenate([attn, conv, ret], axis=-1) @ w_o
    return out, c_kv, k_pe, s_f, s_b


def _swiglu(h, w_gate, w_up, w_down):
    return (jax.nn.silu(h @ w_gate) * (h @ w_up)) @ w_down


def setup_inputs(seed: int = 0) -> dict:
    key = jax.random.key(seed)
    ks = jax.random.split(key, 25)
    f32 = jnp.float32

    def nrm(k, shape, scale=1.0):
        return jax.random.normal(k, shape, f32) * scale

    def gain(k, shape):
        return 1.0 + 0.05 * jax.random.normal(k, shape, f32)

    p = 1.0 - jnp.power(2.0, -5.0 - jnp.arange(RET_HEADS, dtype=f32))
    decay_logit = jnp.log(p) - jnp.log1p(-p)
    return {
        "x_prompt": nrm(ks[0], (BATCH, SEQ, D_MODEL)),
        "x_sample": nrm(ks[1], (DEC_BATCH, DEC_SEQ, D_MODEL)),
        "cache_ckv": nrm(ks[2], (DEC_BATCH, DEPTH, PAST_LEN, KV_LORA)),
        "cache_kpe": nrm(ks[3], (DEC_BATCH, DEPTH, PAST_LEN, QK_ROPE)),
        "state_ret": nrm(ks[4], (DEC_BATCH, DEPTH, 2, RET_HEADS, RET_DK, RET_DV), 0.1),
        "c": nrm(ks[5], (DEC_BATCH, D_MODEL)),
        "c_ctx": nrm(ks[6], (D_MODEL,)),
        "ada_w": nrm(ks[7], (DEPTH, D_MODEL, 6 * D_MODEL), 0.3 * D_MODEL ** -0.5),
        "ada_b": nrm(ks[8], (DEPTH, 6 * D_MODEL), 0.01),
        "norm1_g": gain(ks[9], (DEPTH, D_MODEL)),
        "norm2_g": gain(ks[10], (DEPTH, D_MODEL)),
        "w_in": nrm(ks[11], (DEPTH, D_MODEL, IN_COLS), D_MODEL ** -0.5),
        "q_norm_g": gain(ks[12], (DEPTH, Q_LORA)),
        "kv_norm_g": gain(ks[13], (DEPTH, KV_LORA)),
        "w_q_up": nrm(ks[14], (DEPTH, Q_LORA, MLA_HEADS * QK_HEAD), Q_LORA ** -0.5),
        "w_kv_up": nrm(ks[15], (DEPTH, KV_LORA, MLA_HEADS * (QK_NOPE + V_HEAD)), KV_LORA ** -0.5),
        "q_head_norm_g": gain(ks[16], (DEPTH, QK_HEAD)),
        "k_head_norm_g": gain(ks[17], (DEPTH, QK_HEAD)),
        "conv_w": nrm(ks[18], (DEPTH, 3, CONV_W), 3 ** -0.5),
        "ret_decay_fwd": decay_logit[None, :] + nrm(ks[19], (DEPTH, RET_HEADS), 0.1),
        "ret_decay_bwd": decay_logit[None, :] + nrm(ks[20], (DEPTH, RET_HEADS), 0.1),
        "w_o": nrm(ks[21], (DEPTH, MIX_W, D_MODEL), MIX_W ** -0.5),
        "w_ffn_gate": nrm(ks[22], (DEPTH, D_MODEL, FFN_HIDDEN), D_MODEL ** -0.5),
        "w_ffn_up": nrm(ks[23], (DEPTH, D_MODEL, FFN_HIDDEN), D_MODEL ** -0.5),
        "w_ffn_down": nrm(ks[24], (DEPTH, FFN_HIDDEN, D_MODEL), FFN_HIDDEN ** -0.5),
    }


def reference(x_prompt, x_sample, cache_ckv, cache_kpe, state_ret, c, c_ctx,
              ada_w, ada_b, norm1_g, norm2_g, w_in, q_norm_g, kv_norm_g, w_q_up, w_kv_up,
              q_head_norm_g, k_head_norm_g, conv_w, ret_decay_fwd, ret_decay_bwd, w_o,
              w_ffn_gate, w_ffn_up, w_ffn_down):
    xp, xs = x_prompt, x_sample
    ckv_list, kpe_list, st_list = [], [], []
    for l in range(DEPTH):
        lw = (w_in[l], q_norm_g[l], kv_norm_g[l], w_q_up[l], w_kv_up[l],
              q_head_norm_g[l], k_head_norm_g[l], conv_w[l],
              jax.nn.log_sigmoid(ret_decay_fwd[l].astype(jnp.float32)),
              jax.nn.log_sigmoid(ret_decay_bwd[l].astype(jnp.float32)), w_o[l])
        ffn = (w_ffn_gate[l], w_ffn_up[l], w_ffn_down[l])

        sh1, sc1, g1, sh2, sc2, g2 = _modulation(c_ctx[None, :], ada_w[l], ada_b[l])
        mix, ckv, kpe, s_f, s_b = _token_mixer(_modulate(_rmsnorm(xp, norm1_g[l]), sh1, sc1), *lw)
        xp = xp + g1 * mix
        xp = xp + g2 * _swiglu(_modulate(_rmsnorm(xp, norm2_g[l]), sh2, sc2), *ffn)
        ckv_list.append(ckv)
        kpe_list.append(kpe)
        st_list.append(jnp.stack([s_f, s_b], axis=1).astype(xp.dtype))

        sh1, sc1, g1, sh2, sc2, g2 = _modulation(c, ada_w[l], ada_b[l])
        mix, _, _, _, _ = _token_mixer(_modulate(_rmsnorm(xs, norm1_g[l]), sh1, sc1), *lw,
                                       cache_ckv[:, l], cache_kpe[:, l],
                                       state_ret[:, l, 0], state_ret[:, l, 1])
        xs = xs + g1 * mix
        xs = xs + g2 * _swiglu(_modulate(_rmsnorm(xs, norm2_g[l]), sh2, sc2), *ffn)

    new_ckv = jnp.stack(ckv_list, axis=1)
    new_kpe = jnp.stack(kpe_list, axis=1)
    new_state_ret = jnp.stack(st_list, axis=1)
    return (xp, xs, new_ckv, new_kpe, new_state_ret)
```

```python
import functools

import jax
import jax.numpy as jnp
import numpy as np
from jax import lax
from jax.experimental import pallas as pl
from jax.experimental.pallas import tpu as pltpu

D_MODEL = 1024
DEPTH = 2
GRID_W = 64
MLA_HEADS = 8
Q_LORA = 256
KV_LORA = 128
QK_NOPE = 64
QK_ROPE = 32
V_HEAD = 64
QK_HEAD = QK_NOPE + QK_ROPE
MLA_W = MLA_HEADS * V_HEAD
CONV_W = 256
RET_HEADS = 4
RET_DK = 64
RET_DV = 64
RET_W = RET_HEADS * RET_DV
RET_CHUNK = 128
FFN_HIDDEN = 2816
ROPE_THETA = 10000.0
EPS = 1e-6

LANES = 128
SUBLANES = 8
MXU_N = 256
HEAD_PAD = LANES
QK_W = MLA_HEADS * HEAD_PAD
VMEM_LIMIT = 56 * 1024 * 1024

C_QLAT = 0
C_KVLAT = C_QLAT + Q_LORA
C_KPE = C_KVLAT + KV_LORA
C_GB = C_KPE + LANES
C_GC = C_GB + CONV_W
C_XIN = C_GC + CONV_W
C_RQ = C_XIN + CONV_W
C_RK = C_RQ + RET_W
C_RV = C_RK + RET_W
C_RG = C_RV + RET_W
IN_COLS_PAD = C_RG + RET_W

BF16 = jnp.bfloat16
F32 = jnp.float32


def _dot(a, b):
    return jnp.dot(a, b, preferred_element_type=F32)


def _dot_nt(a, b):
    return lax.dot_general(a, b, (((1,), (1,)), ((), ())), preferred_element_type=F32)


def _rms(x, n=None):
    n = x.shape[-1] if n is None else n
    return x * lax.rsqrt(jnp.sum(x * x, axis=-1, keepdims=True) * (1.0 / n) + EPS)


def _silu(x):
    return x * (1.0 / (1.0 + jnp.exp(-x)))


def _const_spec(shape):
    nd = len(shape)
    return pl.BlockSpec(shape, lambda *_: (0,) * nd, pipeline_mode=pl.Buffered(1))


ADA_TN = 1536


def _ada_kernel(cond_ref, w_ref, b_ref, o_ref):
    a = _silu(cond_ref[...]).astype(BF16)
    o_ref[...] = _dot(a, w_ref[...].astype(BF16)) + b_ref[...]


def _ada_call(cond, ada_w, ada_b):
    n = 6 * D_MODEL
    return pl.pallas_call(
        _ada_kernel,
        grid=(DEPTH, n // ADA_TN),
        in_specs=[
            pl.BlockSpec((SUBLANES, D_MODEL), lambda l, j: (0, 0)),
            pl.BlockSpec((None, D_MODEL, ADA_TN), lambda l, j: (l, 0, j)),
            pl.BlockSpec((None, 1, ADA_TN), lambda l, j: (l, 0, j)),
        ],
        out_specs=pl.BlockSpec((None, SUBLANES, ADA_TN), lambda l, j: (l, 0, j)),
        out_shape=jax.ShapeDtypeStruct((DEPTH, SUBLANES, n), F32),
        compiler_params=pltpu.CompilerParams(
            dimension_semantics=("arbitrary", "arbitrary"), vmem_limit_bytes=VMEM_LIMIT),
        name="ada_mod",
    )(cond, ada_w, ada_b.reshape(DEPTH, 1, n))


def _head_norm_store(kv, kpe_slab, g, out_ref, rope=None):
    for hd in range(MLA_HEADS):
        sl = slice(hd * HEAD_PAD, (hd + 1) * HEAD_PAD)
        slab = kv[:, sl]
        if kpe_slab is not None:
            slab = slab + kpe_slab
        y = _rms(slab, QK_HEAD) * g
        if rope is not None:
            y = _apply_rope(y, *rope)
        out_ref[:, sl] = y.astype(BF16)


def _apply_rope(y, cos, sin_lo, sin_hi):
    return (y * cos + pltpu.roll(y, LANES - 8, axis=1) * sin_lo
            + pltpu.roll(y, 8, axis=1) * sin_hi)


def _ctx_kv_kernel(ckv_ref, kpe_ref, wkv_ref, khg_ref, k_ref, v_ref):
    kv = _dot(ckv_ref[...].astype(BF16), wkv_ref[...])
    _head_norm_store(kv, kpe_ref[...], khg_ref[...], k_ref)
    v_ref[...] = kv[:, QK_W:].astype(BF16)


def _ctx_kv_call(cache_ckv, kpe_slab, wkv, khg):
    b, _, t, _ = cache_ckv.shape
    return pl.pallas_call(
        _ctx_kv_kernel,
        grid=(DEPTH, b),
        in_specs=[
            pl.BlockSpec((None, None, t, KV_LORA), lambda l, i: (i, l, 0, 0)),
            pl.BlockSpec((None, None, t, LANES), lambda l, i: (i, l, 0, 0)),
            pl.BlockSpec((None, KV_LORA, QK_W + MLA_W), lambda l, i: (l, 0, 0)),
            pl.BlockSpec((None, 1, LANES), lambda l, i: (l, 0, 0)),
        ],
        out_specs=[
            pl.BlockSpec((None, None, t, QK_W), lambda l, i: (l, i, 0, 0)),
            pl.BlockSpec((None, None, t, MLA_W), lambda l, i: (l, i, 0, 0)),
        ],
        out_shape=[
            jax.ShapeDtypeStruct((DEPTH, b, t, QK_W), BF16),
            jax.ShapeDtypeStruct((DEPTH, b, t, MLA_W), BF16),
        ],
        compiler_params=pltpu.CompilerParams(
            dimension_semantics=("arbitrary", "arbitrary"), vmem_limit_bytes=VMEM_LIMIT),
        name="ctx_kv",
    )(cache_ckv, kpe_slab, wkv, khg)


def _proj_kernel(*refs, latent, tiles_per_seq, tm):
    it = iter(refs)
    x_ref = next(it)
    if latent:
        xp_ref, xn_ref = next(it), next(it)
    mod_ref, g1_ref, win_ref, qng_ref, kvng_ref = (next(it) for _ in range(5))
    wq_ref, wkv_ref, qhg_ref, khg_ref, cw_ref = (next(it) for _ in range(5))
    if latent:
        cos_ref, slo_ref, shi_ref = next(it), next(it), next(it)
    q_out, k_out, v_out = next(it), next(it), next(it)
    if not latent:
        ckv_out, kpe_out = next(it), next(it)
    conv_out, rq_out, rk_out, rv_out, rg_out = (next(it) for _ in range(5))

    shift = mod_ref[:, 0:D_MODEL]
    scale1 = 1.0 + mod_ref[:, D_MODEL:2 * D_MODEL]
    g1 = g1_ref[...]

    def norm_mod(x):
        return ((_rms(x) * g1) * scale1 + shift).astype(BF16)

    h = norm_mod(x_ref[...])

    def proj(col, width, hh=h):
        return _dot(hh, win_ref[:, col:col + width])

    rope = (cos_ref[...], slo_ref[...], shi_ref[...]) if latent else None
    q_scale = QK_HEAD ** -0.5

    qn = (_rms(proj(C_QLAT, Q_LORA)) * qng_ref[...]).astype(BF16)
    q = _dot(qn, wq_ref[...])
    qhg = qhg_ref[...]
    for hd in range(MLA_HEADS):
        sl = slice(hd * HEAD_PAD, (hd + 1) * HEAD_PAD)
        y = _rms(q[:, sl], QK_HEAD) * qhg
        if latent:
            y = _apply_rope(y, *rope)
        q_out[:, sl] = (y * q_scale).astype(BF16)

    ckv = _rms(proj(C_KVLAT, KV_LORA)) * kvng_ref[...]
    kpe_slab = proj(C_KPE, LANES)
    if not latent:
        ckv_out[...] = ckv
        kpe_out[...] = kpe_slab
    kv = _dot(ckv.astype(BF16), wkv_ref[...])
    _head_norm_store(kv, kpe_slab, khg_ref[...], k_out, rope)
    v_out[...] = kv[:, QK_W:].astype(BF16)

    u = proj(C_GC, CONV_W) * proj(C_XIN, CONV_W)
    row = lax.broadcasted_iota(jnp.int32, (tm, CONV_W), 0)
    if latent:
        i = pl.program_id(0)
        hh = norm_mod(jnp.concatenate([xp_ref[...], xn_ref[...]], axis=0))
        uh = proj(C_GC, CONV_W, hh) * proj(C_XIN, CONV_W, hh)
        has_prev = (i % tiles_per_seq != 0).astype(F32)
        has_next = (i % tiles_per_seq != tiles_per_seq - 1).astype(F32)
        u_before = uh[SUBLANES - 1:SUBLANES, :] * has_prev
        u_after = uh[SUBLANES:SUBLANES + 1, :] * has_next
    else:
        u_before = jnp.zeros((1, CONV_W), F32)
        u_after = jnp.zeros((1, CONV_W), F32)
    u_m1 = jnp.where(row == 0, u_before, pltpu.roll(u, 1, axis=0))
    u_p1 = jnp.where(row == tm - 1, u_after, pltpu.roll(u, tm - 1, axis=0))
    y = u_m1 * cw_ref[0:1, :] + u * cw_ref[1:2, :] + u_p1 * cw_ref[2:3, :]
    conv_out[...] = (proj(C_GB, CONV_W) * y).astype(BF16)

    rq_out[...] = proj(C_RQ, RET_W).astype(BF16)
    rk_out[...] = (proj(C_RK, RET_W) * (RET_DK ** -0.5)).astype(BF16)
    rv_out[...] = proj(C_RV, RET_W).astype(BF16)
    rg_out[...] = proj(C_RG, RET_W)


def _proj_call(x2d, mod, lw, rope_tabs, *, latent, seq, tm):
    rows = x2d.shape[0]
    tiles_per_seq = seq // tm
    n_tiles = rows // tm
    if latent:
        mod_map = lambda i: (1 + i // tiles_per_seq, 0, 0)
    else:
        mod_map = lambda i: (0, 0, 0)
    row_spec = lambda w: pl.BlockSpec((tm, w), lambda i: (i, 0))
    in_specs = [row_spec(D_MODEL)]
    args = [x2d]
    if latent:
        blk = tm // SUBLANES
        last = rows // SUBLANES - 1
        in_specs += [
            pl.BlockSpec((SUBLANES, D_MODEL), lambda i: (jnp.maximum(i * blk - 1, 0), 0)),
            pl.BlockSpec((SUBLANES, D_MODEL), lambda i: (jnp.minimum((i + 1) * blk, last), 0)),
        ]
        args += [x2d, x2d]
    in_specs += [
        pl.BlockSpec((None, 1, 6 * D_MODEL), mod_map),
        _const_spec((1, D_MODEL)),
        _const_spec((D_MODEL, IN_COLS_PAD)),
        _const_spec((1, Q_LORA)),
        _const_spec((1, KV_LORA)),
        _const_spec((Q_LORA, QK_W)),
        _const_spec((KV_LORA, QK_W + MLA_W)),
        _const_spec((1, LANES)),
        _const_spec((1, LANES)),
        _const_spec((SUBLANES, CONV_W)),
    ]
    args += [mod, lw["g1"], lw["w_in"], lw["qng"], lw["kvng"], lw["wq"], lw["wkv"],
             lw["qhg"], lw["khg"], lw["convw"]]
    if latent:
        in_specs += [pl.BlockSpec((tm, LANES), lambda i: (i % tiles_per_seq, 0))] * 3
        args += list(rope_tabs)
    outs = [(QK_W, BF16), (QK_W, BF16), (MLA_W, BF16)]
    if not latent:
        outs += [(KV_LORA, F32), (LANES, F32)]
    outs += [(CONV_W, BF16), (RET_W, BF16), (RET_W, BF16), (RET_W, BF16), (RET_W, F32)]
    return pl.pallas_call(
        functools.partial(_proj_kernel, latent=latent, tiles_per_seq=tiles_per_seq, tm=tm),
        grid=(n_tiles,),
        in_specs=in_specs,
        out_specs=[row_spec(w) for w, _ in outs],
        out_shape=[jax.ShapeDtypeStruct((rows, w), dt) for w, dt in outs],
        compiler_params=pltpu.CompilerParams(
            dimension_semantics=("arbitrary",), vmem_limit_bytes=VMEM_LIMIT),
        name="proj_latent" if latent else "proj_ctx",
    )(*args)


def _attn_kernel(*refs, latent):
    if latent:
        q_ref, k_ref, v_ref, kc_ref, vc_ref, o_ref = refs
    else:
        q_ref, k_ref, v_ref, o_ref = refs
    outs = []
    for j in range(2):
        sl = slice(j * HEAD_PAD, (j + 1) * HEAD_PAD)
        q = q_ref[:, sl]
        s = _dot_nt(q, k_ref[:, sl])
        m = jnp.max(s, axis=-1, keepdims=True)
        if latent:
            sc = _dot_nt(q, kc_ref[:, sl])
            m = jnp.maximum(m, jnp.max(sc, axis=-1, keepdims=True))
        p = jnp.exp(s - m)
        l = jnp.sum(p, axis=-1, keepdims=True)
        o = _dot(p.astype(BF16), v_ref[...])
        if latent:
            pc = jnp.exp(sc - m)
            l = l + jnp.sum(pc, axis=-1, keepdims=True)
            o = o + _dot(pc.astype(BF16), vc_ref[...])
        outs.append(o * (1.0 / l))
    lane = lax.broadcasted_iota(jnp.int32, outs[0].shape, 1)
    o_ref[...] = jnp.where(lane < V_HEAD, outs[0], outs[1]).astype(BF16)


def _attn_call(q, k, v, kc, vc, layer, *, tq):
    b, t, _ = q.shape
    latent = kc is not None
    tk = k.shape[1]
    in_specs = [
        pl.BlockSpec((None, tq, 2 * HEAD_PAD), lambda bi, hp, i: (bi, i, hp)),
        pl.BlockSpec((None, tk, 2 * HEAD_PAD), lambda bi, hp, i: (bi, 0, hp)),
        pl.BlockSpec((None, tk, 2 * V_HEAD), lambda bi, hp, i: (bi, 0, hp)),
    ]
    args = [q, k, v]
    if latent:
        tc = kc.shape[2]
        in_specs += [
            pl.BlockSpec((None, None, tc, 2 * HEAD_PAD), lambda bi, hp, i: (layer, bi, 0, hp)),
            pl.BlockSpec((None, None, tc, 2 * V_HEAD), lambda bi, hp, i: (layer, bi, 0, hp)),
        ]
        args += [kc, vc]
    return pl.pallas_call(
        functools.partial(_attn_kernel, latent=latent),
        grid=(b, MLA_HEADS // 2, t // tq),
        in_specs=in_specs,
        out_specs=pl.BlockSpec((None, tq, 2 * V_HEAD), lambda bi, hp, i: (bi, i, hp)),
        out_shape=jax.ShapeDtypeStruct((b, t, MLA_W), BF16),
        compiler_params=pltpu.CompilerParams(
            dimension_semantics=("arbitrary", "arbitrary", "arbitrary"),
            vmem_limit_bytes=VMEM_LIMIT),
        name="attn_latent" if latent else "attn_ctx",
    )(*args)


def _ret_kernel(*refs, latent, n_chunks):
    if latent:
        lgl_ref, lgc_ref, rq_ref, rk_ref, rv_ref, rg_ref, s0_ref, out_ref, o_acc, st_ref = refs
    else:
        lgl_ref, lgc_ref, rq_ref, rk_ref, rv_ref, rg_ref, out_ref, sf_ref, sb_ref, o_acc, st_ref = refs
    c = RET_CHUNK
    lg_f, lg_b = lgl_ref[0:1, :], lgl_ref[1:2, :]
    pos = lax.broadcasted_iota(jnp.int32, (c, RET_W), 0).astype(F32)
    qdec_f = jnp.exp(lg_f * (pos + 1.0))
    kdec_f = jnp.exp(lg_f * (c - 1.0 - pos))
    qdec_b = jnp.exp(lg_b * (c - pos))
    kdec_b = jnp.exp(lg_b * pos)
    cdec_f = jnp.exp(lg_f * float(c))
    cdec_b = jnp.exp(lg_b * float(c))
    ri = lax.broadcasted_iota(jnp.int32, (c, RET_HEADS * c), 0)
    ci = jnp.bitwise_and(lax.broadcasted_iota(jnp.int32, (c, RET_HEADS * c), 1), c - 1)
    diff = (ri - ci).astype(F32)
    dcat = (jnp.where(diff >= 0, jnp.exp(lgc_ref[0:1, :] * diff), 0.0)
            + jnp.where(diff <= 0, jnp.exp(lgc_ref[1:2, :] * (-diff)), 0.0))
    head_shift = RET_DV.bit_length() - 1
    lane_head = jnp.right_shift(lax.broadcasted_iota(jnp.int32, (c, RET_W), 1), head_shift)
    r_head = jnp.right_shift(lax.broadcasted_iota(jnp.int32, (RET_W, RET_W), 0), head_shift)
    c_head = jnp.right_shift(lax.broadcasted_iota(jnp.int32, (RET_W, RET_W), 1), head_shift)
    same_head = r_head == c_head
    head_mean = jnp.where(same_head, 1.0 / RET_DV, 0.0).astype(BF16)

    def stack_heads(a):
        zero = jnp.zeros_like(a)
        return jnp.concatenate([jnp.where(lane_head == hd, a, zero) for hd in range(RET_HEADS)], axis=0)

    def load(n):
        sl = pl.ds(pl.multiple_of(n * c, c), c)
        return sl, rq_ref[sl, :], rk_ref[sl, :], rv_ref[sl, :]

    def state_step(k, v, kdec, cdec):
        kd = (k.astype(F32) * kdec).T.astype(BF16)
        st_ref[...] = st_ref[...] * cdec + jnp.where(same_head, _dot(kd, v), 0.0)

    if latent:
        st_ref[...] = s0_ref[0]
    else:
        st_ref[...] = jnp.zeros_like(st_ref)

    def fwd(n, carry):
        sl, q, k, v = load(n)
        scores = _dot_nt(q, stack_heads(k))
        o = _dot((scores * dcat).astype(BF16), stack_heads(v))
        o = o + _dot((q.astype(F32) * qdec_f).astype(BF16), st_ref[...].astype(BF16))
        o_acc[sl, :] = o
        state_step(k, v, kdec_f, cdec_f)
        return carry

    lax.fori_loop(0, n_chunks, fwd, 0)
    if not latent:
        sf_ref[...] = st_ref[...]

    if latent:
        st_ref[...] = s0_ref[1]
    else:
        st_ref[...] = jnp.zeros_like(st_ref)

    def bwd(j, carry):
        n = n_chunks - 1 - j
        sl, q, k, v = load(n)
        o = o_acc[sl, :] + _dot((q.astype(F32) * qdec_b).astype(BF16), st_ref[...].astype(BF16))
        state_step(k, v, kdec_b, cdec_b)
        sq = o * o
        hi = sq.astype(BF16)
        lo = (sq - hi.astype(F32)).astype(BF16)
        ms = _dot(hi, head_mean) + _dot(lo, head_mean)
        y = o * lax.rsqrt(ms + EPS)
        out_ref[sl, :] = (y * _silu(rg_ref[sl, :])).astype(BF16)
        return carry

    lax.fori_loop(0, n_chunks, bwd, 0)
    if not latent:
        sb_ref[...] = st_ref[...]


def _ret_call(lgl, lgc, rq, rk, rv, rg, s0):
    b, t, _ = rq.shape
    latent = s0 is not None
    seq_spec = pl.BlockSpec((None, t, RET_W), lambda i: (i, 0, 0))
    st_spec = pl.BlockSpec((None, RET_W, RET_W), lambda i: (i, 0, 0))
    in_specs = [_const_spec((SUBLANES, RET_W)), _const_spec((SUBLANES, RET_HEADS * RET_CHUNK)),
                seq_spec, seq_spec, seq_spec, seq_spec]
    args = [lgl, lgc, rq, rk, rv, rg]
    out_specs = [seq_spec]
    out_shape = [jax.ShapeDtypeStruct((b, t, RET_W), BF16)]
    if latent:
        in_specs.append(pl.BlockSpec((None, 2, RET_W, RET_W), lambda i: (i, 0, 0, 0)))
        args.append(s0)
    else:
        out_specs += [st_spec, st_spec]
        out_shape += [jax.ShapeDtypeStruct((b, RET_W, RET_W), F32)] * 2
    return pl.pallas_call(
        functools.partial(_ret_kernel, latent=latent, n_chunks=t // RET_CHUNK),
        grid=(b,),
        in_specs=in_specs,
        out_specs=out_specs,
        out_shape=out_shape,
        scratch_shapes=[pltpu.VMEM((t, RET_W), F32), pltpu.VMEM((RET_W, RET_W), F32)],
        compiler_params=pltpu.CompilerParams(
            dimension_semantics=("arbitrary",), vmem_limit_bytes=VMEM_LIMIT),
        name="ret_latent" if latent else "ret_ctx",
    )(*args)


def _out_kernel(x_ref, attn_ref, conv_ref, ret_ref, mod_ref, g2_ref, wo_ref, wg_ref, wu_ref,
                wd_ref, o_ref, act_ref):
    mix = (_dot(attn_ref[...], wo_ref[0:MLA_W, :])
           + _dot(conv_ref[...], wo_ref[MLA_W:MLA_W + CONV_W, :])
           + _dot(ret_ref[...], wo_ref[MLA_W + CONV_W:, :]))
    x1 = x_ref[...] + mod_ref[:, 2 * D_MODEL:3 * D_MODEL] * mix
    h = ((_rms(x1) * g2_ref[...]) * (1.0 + mod_ref[:, 4 * D_MODEL:5 * D_MODEL])
         + mod_ref[:, 3 * D_MODEL:4 * D_MODEL]).astype(BF16)
    for j in range(FFN_HIDDEN // MXU_N):
        sl = slice(j * MXU_N, (j + 1) * MXU_N)
        gate = _dot(h, wg_ref[:, sl])
        up = _dot(h, wu_ref[:, sl])
        act_ref[:, sl] = (_silu(gate) * up).astype(BF16)
    ffn = _dot(act_ref[...], wd_ref[...])
    o_ref[...] = x1 + mod_ref[:, 5 * D_MODEL:6 * D_MODEL] * ffn


def _out_call(x2d, attn, conv, ret, mod, lw, *, latent, seq, tm):
    rows = x2d.shape[0]
    tiles_per_seq = seq // tm
    if latent:
        mod_map = lambda i: (1 + i // tiles_per_seq, 0, 0)
    else:
        mod_map = lambda i: (0, 0, 0)
    row_spec = lambda w: pl.BlockSpec((tm, w), lambda i: (i, 0))
    return pl.pallas_call(
        _out_kernel,
        grid=(rows // tm,),
        in_specs=[
            row_spec(D_MODEL), row_spec(MLA_W), row_spec(CONV_W), row_spec(RET_W),
            pl.BlockSpec((None, 1, 6 * D_MODEL), mod_map),
            _const_spec((1, D_MODEL)),
            _const_spec((D_MODEL, D_MODEL)),
            _const_spec((D_MODEL, FFN_HIDDEN)),
            _const_spec((D_MODEL, FFN_HIDDEN)),
            _const_spec((FFN_HIDDEN, D_MODEL)),
        ],
        out_specs=row_spec(D_MODEL),
        out_shape=jax.ShapeDtypeStruct((rows, D_MODEL), F32),
        scratch_shapes=[pltpu.VMEM((tm, FFN_HIDDEN), BF16)],
        compiler_params=pltpu.CompilerParams(
            dimension_semantics=("arbitrary",), vmem_limit_bytes=VMEM_LIMIT),
        name="out_latent" if latent else "out_ctx",
    )(x2d, attn, conv, ret, mod, lw["g2"], lw["wo"], lw["wg"], lw["wu"], lw["wd"])


def _pad_heads(w, heads, width):
    lead = w.shape[:-1]
    w = w.reshape(*lead, heads, width)
    w = jnp.pad(w, [(0, 0)] * len(lead) + [(0, 0), (0, HEAD_PAD - width)])
    return w.reshape(*lead, heads * HEAD_PAD)


def _layer_weights(l, w_in, q_norm_g, kv_norm_g, w_q_up, w_kv_up, q_head_norm_g, k_head_norm_g,
                   conv_w, norm1_g, norm2_g, w_o, w_ffn_gate, w_ffn_up, w_ffn_down):
    wi = w_in[l]
    kpe_cols = jnp.pad(wi[:, C_KPE:C_KPE + QK_ROPE], ((0, 0), (QK_NOPE, LANES - QK_HEAD)))
    w_in_p = jnp.concatenate([wi[:, :C_KPE], kpe_cols, wi[:, C_KPE + QK_ROPE:]], axis=1)
    kvu = w_kv_up[l].reshape(KV_LORA, MLA_HEADS, QK_NOPE + V_HEAD)
    wk = _pad_heads(kvu[:, :, :QK_NOPE].reshape(KV_LORA, MLA_HEADS * QK_NOPE), MLA_HEADS, QK_NOPE)
    wv = kvu[:, :, QK_NOPE:].reshape(KV_LORA, MLA_W)
    return {
        "g1": norm1_g[l].reshape(1, D_MODEL),
        "g2": norm2_g[l].reshape(1, D_MODEL),
        "w_in": w_in_p.astype(BF16),
        "qng": q_norm_g[l].reshape(1, Q_LORA),
        "kvng": kv_norm_g[l].reshape(1, KV_LORA),
        "wq": _pad_heads(w_q_up[l], MLA_HEADS, QK_HEAD).astype(BF16),
        "wkv": jnp.concatenate([wk, wv], axis=1).astype(BF16),
        "qhg": jnp.pad(q_head_norm_g[l], (0, HEAD_PAD - QK_HEAD)).reshape(1, LANES),
        "khg": jnp.pad(k_head_norm_g[l], (0, HEAD_PAD - QK_HEAD)).reshape(1, LANES),
        "convw": jnp.pad(conv_w[l], ((0, SUBLANES - 3), (0, 0))),
        "wo": w_o[l].astype(BF16),
        "wg": w_ffn_gate[l].astype(BF16),
        "wu": w_ffn_up[l].astype(BF16),
        "wd": w_ffn_down[l].astype(BF16),
    }


def _rope_tables(t):
    rows = t // GRID_W
    row = jnp.repeat(jnp.arange(rows), GRID_W).astype(F32)
    col = jnp.tile(jnp.arange(GRID_W), rows).astype(F32)
    half = QK_ROPE // 2
    freqs = jnp.power(ROPE_THETA, -jnp.arange(0, half, 2, dtype=F32) / half)
    ang_r = row[:, None] * freqs[None, :]
    ang_c = col[:, None] * freqs[None, :]
    zeros = jnp.zeros((t, half // 2), F32)
    lead1 = jnp.ones((t, QK_NOPE), F32)
    lead0 = jnp.zeros((t, QK_NOPE), F32)
    tail1 = jnp.ones((t, HEAD_PAD - QK_HEAD), F32)
    tail0 = jnp.zeros((t, HEAD_PAD - QK_HEAD), F32)
    cos = jnp.concatenate([lead1, jnp.cos(ang_r), jnp.cos(ang_r), jnp.cos(ang_c), jnp.cos(ang_c), tail1], axis=1)
    sin_lo = jnp.concatenate([lead0, -jnp.sin(ang_r), zeros, -jnp.sin(ang_c), zeros, tail0], axis=1)
    sin_hi = jnp.concatenate([lead0, zeros, jnp.sin(ang_r), zeros, jnp.sin(ang_c), tail0], axis=1)
    return cos, sin_lo, sin_hi


def _decay_lanes(ret_decay_fwd, ret_decay_bwd, l):
    lg = jnp.stack([jax.nn.log_sigmoid(ret_decay_fwd[l].astype(F32)),
                    jax.nn.log_sigmoid(ret_decay_bwd[l].astype(F32))])
    pad = ((0, SUBLANES - 2), (0, 0))
    return (jnp.pad(jnp.repeat(lg, RET_DV, axis=1), pad),
            jnp.pad(jnp.repeat(lg, RET_CHUNK, axis=1), pad))


def _block_diag(s):
    eye = jnp.eye(RET_HEADS, dtype=s.dtype)
    out = jnp.einsum("...hkv,hg->...hkgv", s, eye)
    return out.reshape(*s.shape[:-3], RET_HEADS * RET_DK, RET_HEADS * RET_DV)


def _diag_blocks(s):
    b = s.shape[0]
    s = s.reshape(b, RET_HEADS, RET_DK, RET_HEADS, RET_DV)
    return jnp.stack([s[:, hd, :, hd, :] for hd in range(RET_HEADS)], axis=1)


TM = 256
TQ = 256


def kernel(x_prompt, x_sample, cache_ckv, cache_kpe, state_ret, c, c_ctx, ada_w, ada_b, norm1_g, norm2_g, w_in, q_norm_g, kv_norm_g, w_q_up, w_kv_up, q_head_norm_g, k_head_norm_g, conv_w, ret_decay_fwd, ret_decay_bwd, w_o, w_ffn_gate, w_ffn_up, w_ffn_down):
    bp, tp, _ = x_prompt.shape
    bs, ts, _ = x_sample.shape

    cond = jnp.concatenate([c_ctx[None, :], c, jnp.zeros((SUBLANES - 1 - bs, D_MODEL), F32)], axis=0)
    mods = _ada_call(cond, ada_w, ada_b).reshape(DEPTH, SUBLANES, 1, 6 * D_MODEL)

    lws = [_layer_weights(l, w_in, q_norm_g, kv_norm_g, w_q_up, w_kv_up, q_head_norm_g,
                          k_head_norm_g, conv_w, norm1_g, norm2_g, w_o, w_ffn_gate, w_ffn_up,
                          w_ffn_down) for l in range(DEPTH)]
    rope_tabs = _rope_tables(ts)
    kpe_ctx = jnp.pad(cache_kpe, ((0, 0), (0, 0), (0, 0), (QK_NOPE, LANES - QK_HEAD)))
    kc, vc = _ctx_kv_call(cache_ckv, kpe_ctx,
                          jnp.stack([lw["wkv"] for lw in lws]),
                          jnp.stack([lw["khg"] for lw in lws]))
    s0 = _block_diag(state_ret)

    xp = x_prompt.reshape(bp * tp, D_MODEL)
    xs = x_sample.reshape(bs * ts, D_MODEL)
    ckv_list, kpe_list, st_list = [], [], []
    for l in range(DEPTH):
        lw, mod = lws[l], mods[l]
        lgl, lgc = _decay_lanes(ret_decay_fwd, ret_decay_bwd, l)

        q, k, v, ckv, kpe, conv, rq, rk, rv, rg = _proj_call(
            xp, mod, lw, None, latent=False, seq=tp, tm=TM)
        attn = _attn_call(q.reshape(bp, tp, QK_W), k.reshape(bp, tp, QK_W),
                          v.reshape(bp, tp, MLA_W), None, None, l, tq=TQ)
        r3 = lambda a: a.reshape(bp, tp, RET_W)
        ret, s_f, s_b = _ret_call(lgl, lgc, r3(rq), r3(rk), r3(rv), r3(rg), None)
        xp = _out_call(xp, attn.reshape(bp * tp, MLA_W), conv, ret.reshape(bp * tp, RET_W),
                       mod, lw, latent=False, seq=tp, tm=TM)
        ckv_list.append(ckv.reshape(bp, tp, KV_LORA))
        kpe_list.append(kpe.reshape(bp, tp, LANES)[:, :, QK_NOPE:QK_HEAD])
        st_list.append(jnp.stack([_diag_blocks(s_f), _diag_blocks(s_b)], axis=1))

        q, k, v, conv, rq, rk, rv, rg = _proj_call(
            xs, mod, lw, rope_tabs, latent=True, seq=ts, tm=TM)
        attn = _attn_call(q.reshape(bs, ts, QK_W), k.reshape(bs, ts, QK_W),
                          v.reshape(bs, ts, MLA_W), kc, vc, l, tq=TQ)
        r3 = lambda a: a.reshape(bs, ts, RET_W)
        (ret,) = _ret_call(lgl, lgc, r3(rq), r3(rk), r3(rv), r3(rg), s0[:, l])
        xs = _out_call(xs, attn.reshape(bs * ts, MLA_W), conv, ret.reshape(bs * ts, RET_W),
                       mod, lw, latent=True, seq=ts, tm=TM)

    return (xp.reshape(bp, tp, D_MODEL), xs.reshape(bs, ts, D_MODEL),
            jnp.stack(ckv_list, axis=1), jnp.stack(kpe_list, axis=1), jnp.stack(st_list, axis=1))
```

```python
import functools

import jax
import jax.numpy as jnp
from jax import lax
from jax.experimental import pallas as pl
from jax.experimental.pallas import tpu as pltpu

D_MODEL = 1024
DEPTH = 2
GRID_W = 64
MLA_HEADS = 8
Q_LORA = 256
KV_LORA = 128
QK_NOPE = 64
QK_ROPE = 32
V_HEAD = 64
QK_HEAD = QK_NOPE + QK_ROPE
MLA_W = MLA_HEADS * V_HEAD
CONV_W = 256
RET_HEADS = 4
RET_DK = 64
RET_DV = 64
RET_W = RET_HEADS * RET_DV
RET_CHUNK = 128
FFN_HIDDEN = 2816
ROPE_THETA = 10000.0
EPS = 1e-6

LANES = 128
SUBLANES = 8
MXU_N = 256
HEAD_PAD = LANES
QK_W = MLA_HEADS * HEAD_PAD
ROPE_PAIR = QK_ROPE // 4
VMEM_LIMIT = 56 * 1024 * 1024

C_QLAT = 0
C_KVLAT = C_QLAT + Q_LORA
C_KPE = C_KVLAT + KV_LORA
C_GB = C_KPE + LANES
C_GC = C_GB + CONV_W
C_XIN = C_GC + CONV_W
C_RQ = C_XIN + CONV_W
C_RK = C_RQ + RET_W
C_RV = C_RK + RET_W
C_RG = C_RV + RET_W
IN_COLS_PAD = C_RG + RET_W

BF16 = jnp.bfloat16
F32 = jnp.float32


def _dot(a, b):
    return jnp.dot(a, b, preferred_element_type=F32)


def _dot_nt(a, b):
    return lax.dot_general(a, b, (((1,), (1,)), ((), ())), preferred_element_type=F32)


def _rms(x, n=None):
    n = x.shape[-1] if n is None else n
    return x * lax.rsqrt(jnp.sum(x * x, axis=-1, keepdims=True) * (1.0 / n) + EPS)


def _silu(x):
    return x * (1.0 / (1.0 + jnp.exp(-x)))


def _const_spec(shape):
    nd = len(shape)
    return pl.BlockSpec(shape, lambda *_: (0,) * nd, pipeline_mode=pl.Buffered(1))


def _layer_spec(shape, layer):
    nd = len(shape)
    return pl.BlockSpec((None,) + tuple(shape), lambda *_: (layer,) + (0,) * nd,
                        pipeline_mode=pl.Buffered(1))


def _softmax_pv(s_list, v_list):
    m = functools.reduce(jnp.maximum, [jnp.max(s, axis=-1, keepdims=True) for s in s_list])
    l, o = None, None
    for s, v in zip(s_list, v_list):
        p = jnp.exp(s - m)
        ls = jnp.sum(p, axis=-1, keepdims=True)
        os_ = _dot(p.astype(BF16), v)
        l = ls if l is None else l + ls
        o = os_ if o is None else o + os_
    return o * (1.0 / l)


def _merge_head_pair(o0, o1):
    lane = lax.broadcasted_iota(jnp.int32, o0.shape, 1)
    return jnp.where(lane < V_HEAD, o0, o1).astype(BF16)


ADA_TN = 1536


def _ada_kernel(cond_ref, w_ref, b_ref, o_ref):
    a = _silu(cond_ref[...]).astype(BF16)
    o_ref[...] = _dot(a, w_ref[...].astype(BF16)) + b_ref[...]


def _ada_call(cond, ada_w, ada_b):
    n = 6 * D_MODEL
    return pl.pallas_call(
        _ada_kernel,
        grid=(DEPTH, n // ADA_TN),
        in_specs=[
            pl.BlockSpec((SUBLANES, D_MODEL), lambda l, j: (0, 0)),
            pl.BlockSpec((None, D_MODEL, ADA_TN), lambda l, j: (l, 0, j)),
            pl.BlockSpec((None, 1, ADA_TN), lambda l, j: (l, 0, j)),
        ],
        out_specs=pl.BlockSpec((None, SUBLANES, ADA_TN), lambda l, j: (l, 0, j)),
        out_shape=jax.ShapeDtypeStruct((DEPTH, SUBLANES, n), F32),
        compiler_params=pltpu.CompilerParams(
            dimension_semantics=("arbitrary", "arbitrary"), vmem_limit_bytes=VMEM_LIMIT),
        name="ada_mod",
    )(cond, ada_w, ada_b.reshape(DEPTH, 1, n))


def _key_slab(kn, shared, pe_ssq, g):
    ssq = jnp.sum(kn * kn, axis=-1, keepdims=True) + pe_ssq
    return ((kn * g + shared) * lax.rsqrt(ssq * (1.0 / QK_HEAD) + EPS)).astype(BF16)


def _ctx_kv_kernel(ckv_ref, kpe_ref, wkv_ref, khg_ref, k_ref, v_ref):
    kv = _dot(ckv_ref[...].astype(BF16), wkv_ref[...])
    kpe = kpe_ref[...]
    g = khg_ref[...]
    shared = kpe * g
    pe_ssq = jnp.sum(kpe * kpe, axis=-1, keepdims=True)
    for hd in range(MLA_HEADS):
        sl = slice(hd * HEAD_PAD, (hd + 1) * HEAD_PAD)
        k_ref[:, sl] = _key_slab(kv[:, sl], shared, pe_ssq, g)
    v_ref[...] = kv[:, QK_W:].astype(BF16)


def _ctx_kv_call(cache_ckv, kpe_slab, wkv, khg):
    b, _, t, _ = cache_ckv.shape
    return pl.pallas_call(
        _ctx_kv_kernel,
        grid=(DEPTH, b),
        in_specs=[
            pl.BlockSpec((None, None, t, KV_LORA), lambda l, i: (i, l, 0, 0)),
            pl.BlockSpec((None, None, t, LANES), lambda l, i: (i, l, 0, 0)),
            pl.BlockSpec((None, KV_LORA, QK_W + MLA_W), lambda l, i: (l, 0, 0)),
            pl.BlockSpec((None, 1, LANES), lambda l, i: (l, 0, 0)),
        ],
        out_specs=[
            pl.BlockSpec((None, None, t, QK_W), lambda l, i: (l, i, 0, 0)),
            pl.BlockSpec((None, None, t, MLA_W), lambda l, i: (l, i, 0, 0)),
        ],
        out_shape=[
            jax.ShapeDtypeStruct((DEPTH, b, t, QK_W), BF16),
            jax.ShapeDtypeStruct((DEPTH, b, t, MLA_W), BF16),
        ],
        compiler_params=pltpu.CompilerParams(
            dimension_semantics=("arbitrary", "arbitrary"), vmem_limit_bytes=VMEM_LIMIT),
        name="ctx_kv",
    )(cache_ckv, kpe_slab, wkv, khg)


def _proj_kernel(*refs, latent, tiles_per_seq, tm):
    it = iter(refs)
    x_ref = next(it)
    if latent:
        xp_ref, xn_ref = next(it), next(it)
    mod_ref, g1_ref, win_ref, qng_ref, kvng_ref = (next(it) for _ in range(5))
    wq_ref, wkv_ref, qhg_ref, khg_ref, cw_ref = (next(it) for _ in range(5))
    if latent:
        cos_ref, slo_ref, shi_ref = next(it), next(it), next(it)
        q_out, k_out, v_out = next(it), next(it), next(it)
    else:
        attn_out, ckv_out, kpe_out = next(it), next(it), next(it)
    conv_out, rq_out, rk_out, rv_out, rg_out = (next(it) for _ in range(5))

    shift = mod_ref[:, 0:D_MODEL]
    scale1 = 1.0 + mod_ref[:, D_MODEL:2 * D_MODEL]
    g1 = g1_ref[...]

    def norm_mod(x):
        return ((_rms(x) * g1) * scale1 + shift).astype(BF16)

    h = norm_mod(x_ref[...])

    def proj(col, width, hh=h):
        return _dot(hh, win_ref[:, col:col + width])

    q_scale = QK_HEAD ** -0.5
    qg = qhg_ref[0:1, :] * q_scale
    kg = khg_ref[...]

    qn = (_rms(proj(C_QLAT, Q_LORA)) * qng_ref[...]).astype(BF16)
    q = _dot(qn, wq_ref[:, 0:QK_W])
    if latent:
        q_sw = _dot(qn, wq_ref[:, QK_W:2 * QK_W])
        cos = cos_ref[...]
        sin = slo_ref[...] + shi_ref[...]
        q_tab = qg * cos
        q_tab_sw = (qhg_ref[1:2, :] * q_scale) * sin

    def query_slab(sl):
        qs = q[:, sl]
        r = lax.rsqrt(jnp.sum(qs * qs, axis=-1, keepdims=True) * (1.0 / QK_HEAD) + EPS)
        if latent:
            return ((qs * q_tab + q_sw[:, sl] * q_tab_sw) * r).astype(BF16)
        return (qs * qg * r).astype(BF16)

    ckv = _rms(proj(C_KVLAT, KV_LORA)) * kvng_ref[...]
    kpe = proj(C_KPE, LANES)
    kv = _dot(ckv.astype(BF16), wkv_ref[...])
    shared = kpe * kg
    if latent:
        shared = (shared * cos + pltpu.roll(shared, LANES - ROPE_PAIR, axis=1) * slo_ref[...]
                  + pltpu.roll(shared, ROPE_PAIR, axis=1) * shi_ref[...])
    pe_ssq = jnp.sum(kpe * kpe, axis=-1, keepdims=True)

    if latent:
        for hd in range(MLA_HEADS):
            sl = slice(hd * HEAD_PAD, (hd + 1) * HEAD_PAD)
            q_out[:, sl] = query_slab(sl)
            k_out[:, sl] = _key_slab(kv[:, sl], shared, pe_ssq, kg)
        v_out[...] = kv[:, QK_W:].astype(BF16)
    else:
        ckv_out[...] = ckv
        kpe_out[...] = kpe[:, QK_NOPE:QK_HEAD]
        for hp in range(MLA_HEADS // 2):
            v_pair = kv[:, QK_W + hp * LANES:QK_W + (hp + 1) * LANES].astype(BF16)
            o = []
            for hd in (2 * hp, 2 * hp + 1):
                sl = slice(hd * HEAD_PAD, (hd + 1) * HEAD_PAD)
                s = _dot_nt(query_slab(sl), _key_slab(kv[:, sl], shared, pe_ssq, kg))
                o.append(_softmax_pv([s], [v_pair]))
            attn_out[:, hp * LANES:(hp + 1) * LANES] = _merge_head_pair(*o)

    u = proj(C_GC, CONV_W) * proj(C_XIN, CONV_W)
    row = lax.broadcasted_iota(jnp.int32, (tm, CONV_W), 0)
    if latent:
        i = pl.program_id(0)
        hh = norm_mod(jnp.concatenate([xp_ref[...], xn_ref[...]], axis=0))
        uh = proj(C_GC, CONV_W, hh) * proj(C_XIN, CONV_W, hh)
        has_prev = jnp.where(i % tiles_per_seq != 0, 1.0, 0.0)
        has_next = jnp.where(i % tiles_per_seq != tiles_per_seq - 1, 1.0, 0.0)
        u_before = uh[SUBLANES - 1:SUBLANES, :] * has_prev
        u_after = uh[SUBLANES:SUBLANES + 1, :] * has_next
    else:
        u_before = jnp.zeros((1, CONV_W), F32)
        u_after = jnp.zeros((1, CONV_W), F32)
    u_m1 = jnp.where(row == 0, u_before, pltpu.roll(u, 1, axis=0))
    u_p1 = jnp.where(row == tm - 1, u_after, pltpu.roll(u, tm - 1, axis=0))
    y = u_m1 * cw_ref[0:1, :] + u * cw_ref[1:2, :] + u_p1 * cw_ref[2:3, :]
    conv_out[...] = (proj(C_GB, CONV_W) * y).astype(BF16)

    rq_out[...] = proj(C_RQ, RET_W).astype(BF16)
    rk_out[...] = (proj(C_RK, RET_W) * (RET_DK ** -0.5)).astype(BF16)
    rv_out[...] = proj(C_RV, RET_W).astype(BF16)
    rg_out[...] = proj(C_RG, RET_W)


def _proj_call(x2d, mod, pw, layer, rope_tabs, *, latent, seq, tm):
    rows = x2d.shape[0]
    tiles_per_seq = seq // tm
    n_tiles = rows // tm
    if latent:
        mod_map = lambda i: (layer, 1 + i // tiles_per_seq, 0, 0)
    else:
        assert tiles_per_seq == 1, "the context path runs its attention inside the proj tile"
        mod_map = lambda i: (layer, 0, 0, 0)
    row_spec = lambda w: pl.BlockSpec((tm, w), lambda i: (i, 0))
    in_specs = [row_spec(D_MODEL)]
    args = [x2d]
    if latent:
        blk = tm // SUBLANES
        last = rows // SUBLANES - 1
        in_specs += [
            pl.BlockSpec((SUBLANES, D_MODEL), lambda i: (jnp.maximum(i * blk - 1, 0), 0)),
            pl.BlockSpec((SUBLANES, D_MODEL), lambda i: (jnp.minimum((i + 1) * blk, last), 0)),
        ]
        args += [x2d, x2d]
    in_specs += [
        pl.BlockSpec((None, None, 1, 6 * D_MODEL), mod_map),
        _layer_spec((1, D_MODEL), layer),
        _layer_spec((D_MODEL, IN_COLS_PAD), layer),
        _layer_spec((1, Q_LORA), layer),
        _layer_spec((1, KV_LORA), layer),
        _layer_spec((Q_LORA, 2 * QK_W), layer),
        _layer_spec((KV_LORA, QK_W + MLA_W), layer),
        _layer_spec((SUBLANES, LANES), layer),
        _layer_spec((1, LANES), layer),
        _layer_spec((SUBLANES, CONV_W), layer),
    ]
    args += [mod, pw["g1"], pw["w_in"], pw["qng"], pw["kvng"], pw["wq"], pw["wkv"],
             pw["qhg"], pw["khg"], pw["convw"]]
    if latent:
        in_specs += [pl.BlockSpec((tm, LANES), lambda i: (i % tiles_per_seq, 0))] * 3
        args += list(rope_tabs)
        outs = [(QK_W, BF16), (QK_W, BF16), (MLA_W, BF16)]
    else:
        outs = [(MLA_W, BF16), (KV_LORA, F32), (QK_ROPE, F32)]
    outs += [(CONV_W, BF16), (RET_W, BF16), (RET_W, BF16), (RET_W, BF16), (RET_W, F32)]
    return pl.pallas_call(
        functools.partial(_proj_kernel, latent=latent, tiles_per_seq=tiles_per_seq, tm=tm),
        grid=(n_tiles,),
        in_specs=in_specs,
        out_specs=[row_spec(w) for w, _ in outs],
        out_shape=[jax.ShapeDtypeStruct((rows, w), dt) for w, dt in outs],
        compiler_params=pltpu.CompilerParams(
            dimension_semantics=("arbitrary",), vmem_limit_bytes=VMEM_LIMIT),
        name="proj_latent" if latent else "proj_ctx",
    )(*args)


def _attn_kernel(q_ref, k_ref, v_ref, kc_ref, vc_ref, o_ref):
    o = []
    for j in range(2):
        sl = slice(j * HEAD_PAD, (j + 1) * HEAD_PAD)
        q = q_ref[:, sl]
        s = [_dot_nt(q, k_ref[:, sl]), _dot_nt(q, kc_ref[:, sl])]
        o.append(_softmax_pv(s, [v_ref[...], vc_ref[...]]))
    o_ref[...] = _merge_head_pair(*o)


def _attn_call(q, k, v, kc, vc, layer, *, tq):
    b, t, _ = q.shape
    tk, tc = k.shape[1], kc.shape[2]
    return pl.pallas_call(
        _attn_kernel,
        grid=(b, MLA_HEADS // 2, t // tq),
        in_specs=[
            pl.BlockSpec((None, tq, 2 * HEAD_PAD), lambda bi, hp, i: (bi, i, hp)),
            pl.BlockSpec((None, tk, 2 * HEAD_PAD), lambda bi, hp, i: (bi, 0, hp)),
            pl.BlockSpec((None, tk, 2 * V_HEAD), lambda bi, hp, i: (bi, 0, hp)),
            pl.BlockSpec((None, None, tc, 2 * HEAD_PAD), lambda bi, hp, i: (layer, bi, 0, hp)),
            pl.BlockSpec((None, None, tc, 2 * V_HEAD), lambda bi, hp, i: (layer, bi, 0, hp)),
        ],
        out_specs=pl.BlockSpec((None, tq, 2 * V_HEAD), lambda bi, hp, i: (bi, i, hp)),
        out_shape=jax.ShapeDtypeStruct((b, t, MLA_W), BF16),
        compiler_params=pltpu.CompilerParams(
            dimension_semantics=("arbitrary", "arbitrary", "arbitrary"),
            vmem_limit_bytes=VMEM_LIMIT),
        name="attn_latent",
    )(q, k, v, kc, vc)


def _ret_kernel(*refs, latent, n_chunks):
    if latent:
        lgl_ref, lgc_ref, rq_ref, rk_ref, rv_ref, rg_ref, s0_ref, out_ref, o_acc, st_ref = refs
    else:
        lgl_ref, lgc_ref, rq_ref, rk_ref, rv_ref, rg_ref, out_ref, sfin_ref, o_acc, st_ref = refs
    c = RET_CHUNK
    lg_f, lg_b = lgl_ref[0:1, :], lgl_ref[1:2, :]
    pos = lax.broadcasted_iota(jnp.int32, (c, RET_W), 0).astype(F32)
    qdec_f = jnp.exp(lg_f * (pos + 1.0))
    kdec_f = jnp.exp(lg_f * (c - 1.0 - pos))
    qdec_b = jnp.exp(lg_b * (c - pos))
    kdec_b = jnp.exp(lg_b * pos)
    cdec_f = jnp.exp(lg_f * float(c))
    cdec_b = jnp.exp(lg_b * float(c))
    ri = lax.broadcasted_iota(jnp.int32, (c, RET_HEADS * c), 0)
    ci = jnp.bitwise_and(lax.broadcasted_iota(jnp.int32, (c, RET_HEADS * c), 1), c - 1)
    diff = (ri - ci).astype(F32)
    dcat = (jnp.where(diff >= 0, jnp.exp(lgc_ref[0:1, :] * diff), 0.0)
            + jnp.where(diff <= 0, jnp.exp(lgc_ref[1:2, :] * (-diff)), 0.0))
    head_shift = RET_DV.bit_length() - 1
    lane_head = jnp.right_shift(lax.broadcasted_iota(jnp.int32, (c, RET_W), 1), head_shift)
    r_head = jnp.right_shift(lax.broadcasted_iota(jnp.int32, (RET_W, RET_W), 0), head_shift)
    c_head = jnp.right_shift(lax.broadcasted_iota(jnp.int32, (RET_W, RET_W), 1), head_shift)
    same_head = r_head == c_head
    head_mean = jnp.where(same_head, 1.0 / RET_DV, 0.0).astype(BF16)

    def stack_heads(a):
        zero = jnp.zeros_like(a)
        return jnp.concatenate([jnp.where(lane_head == hd, a, zero) for hd in range(RET_HEADS)], axis=0)

    def load(n):
        sl = pl.ds(pl.multiple_of(n * c, c), c)
        return sl, rq_ref[sl, :], rk_ref[sl, :], rv_ref[sl, :]

    def state_step(k, v, kdec, cdec):
        kd = (k.astype(F32) * kdec).T.astype(BF16)
        st_ref[...] = st_ref[...] * cdec + jnp.where(same_head, _dot(kd, v), 0.0)

    def init_state(d):
        st_ref[...] = jnp.zeros_like(st_ref)
        if latent:
            for hd in range(RET_HEADS):
                st_ref[hd * RET_DK:(hd + 1) * RET_DK, hd * RET_DV:(hd + 1) * RET_DV] = s0_ref[d, hd]

    def emit_state(d):
        for hd in range(RET_HEADS):
            sfin_ref[d, hd] = st_ref[hd * RET_DK:(hd + 1) * RET_DK, hd * RET_DV:(hd + 1) * RET_DV]

    init_state(0)

    def fwd(n, carry):
        sl, q, k, v = load(n)
        scores = _dot_nt(q, stack_heads(k))
        o = _dot((scores * dcat).astype(BF16), stack_heads(v))
        o = o + _dot((q.astype(F32) * qdec_f).astype(BF16), st_ref[...].astype(BF16))
        o_acc[sl, :] = o
        state_step(k, v, kdec_f, cdec_f)
        return carry

    lax.fori_loop(0, n_chunks, fwd, 0)
    if not latent:
        emit_state(0)
    init_state(1)

    def bwd(j, carry):
        n = n_chunks - 1 - j
        sl, q, k, v = load(n)
        o = o_acc[sl, :] + _dot((q.astype(F32) * qdec_b).astype(BF16), st_ref[...].astype(BF16))
        state_step(k, v, kdec_b, cdec_b)
        sq = o * o
        hi = sq.astype(BF16)
        lo = (sq - hi.astype(F32)).astype(BF16)
        ms = _dot(hi, head_mean) + _dot(lo, head_mean)
        y = o * lax.rsqrt(ms + EPS)
        out_ref[sl, :] = (y * _silu(rg_ref[sl, :])).astype(BF16)
        return carry

    lax.fori_loop(0, n_chunks, bwd, 0)
    if not latent:
        emit_state(1)


def _ret_call(lgl, lgc, rq, rk, rv, rg, state_ret, layer):
    b, t, _ = rq.shape
    latent = state_ret is not None
    seq_spec = pl.BlockSpec((None, t, RET_W), lambda i: (i, 0, 0))
    st_block = (2, RET_HEADS, RET_DK, RET_DV)
    in_specs = [_layer_spec((SUBLANES, RET_W), layer),
                _layer_spec((SUBLANES, RET_HEADS * RET_CHUNK), layer),
                seq_spec, seq_spec, seq_spec, seq_spec]
    args = [lgl, lgc, rq, rk, rv, rg]
    out_specs = [seq_spec]
    out_shape = [jax.ShapeDtypeStruct((b, t, RET_W), BF16)]
    if latent:
        in_specs.append(pl.BlockSpec((None, None) + st_block, lambda i: (i, layer, 0, 0, 0, 0)))
        args.append(state_ret)
    else:
        out_specs.append(pl.BlockSpec((None,) + st_block, lambda i: (i, 0, 0, 0, 0)))
        out_shape.append(jax.ShapeDtypeStruct((b,) + st_block, F32))
    return pl.pallas_call(
        functools.partial(_ret_kernel, latent=latent, n_chunks=t // RET_CHUNK),
        grid=(b,),
        in_specs=in_specs,
        out_specs=out_specs,
        out_shape=out_shape,
        scratch_shapes=[pltpu.VMEM((t, RET_W), F32), pltpu.VMEM((RET_W, RET_W), F32)],
        compiler_params=pltpu.CompilerParams(
            dimension_semantics=("arbitrary",), vmem_limit_bytes=VMEM_LIMIT),
        name="ret_latent" if latent else "ret_ctx",
    )(*args)


def _out_kernel(x_ref, attn_ref, conv_ref, ret_ref, mod_ref, g2_ref, wo_ref, wg_ref, wu_ref,
                wd_ref, o_ref, act_ref):
    mix = (_dot(attn_ref[...], wo_ref[0:MLA_W, :])
           + _dot(conv_ref[...], wo_ref[MLA_W:MLA_W + CONV_W, :])
           + _dot(ret_ref[...], wo_ref[MLA_W + CONV_W:, :]))
    x1 = x_ref[...] + mod_ref[:, 2 * D_MODEL:3 * D_MODEL] * mix
    h = ((_rms(x1) * g2_ref[...]) * (1.0 + mod_ref[:, 4 * D_MODEL:5 * D_MODEL])
         + mod_ref[:, 3 * D_MODEL:4 * D_MODEL]).astype(BF16)
    for j in range(FFN_HIDDEN // MXU_N):
        sl = slice(j * MXU_N, (j + 1) * MXU_N)
        gate = _dot(h, wg_ref[:, sl])
        up = _dot(h, wu_ref[:, sl])
        act_ref[:, sl] = (_silu(gate) * up).astype(BF16)
    ffn = _dot(act_ref[...], wd_ref[...])
    o_ref[...] = x1 + mod_ref[:, 5 * D_MODEL:6 * D_MODEL] * ffn


def _out_call(x2d, attn, conv, ret, mod, pw, layer, *, latent, seq, tm):
    rows = x2d.shape[0]
    tiles_per_seq = seq // tm
    if latent:
        mod_map = lambda i: (layer, 1 + i // tiles_per_seq, 0, 0)
    else:
        mod_map = lambda i: (layer, 0, 0, 0)
    row_spec = lambda w: pl.BlockSpec((tm, w), lambda i: (i, 0))
    return pl.pallas_call(
        _out_kernel,
        grid=(rows // tm,),
        in_specs=[
            row_spec(D_MODEL), row_spec(MLA_W), row_spec(CONV_W), row_spec(RET_W),
            pl.BlockSpec((None, None, 1, 6 * D_MODEL), mod_map),
            _layer_spec((1, D_MODEL), layer),
            _layer_spec((D_MODEL, D_MODEL), layer),
            _layer_spec((D_MODEL, FFN_HIDDEN), layer),
            _layer_spec((D_MODEL, FFN_HIDDEN), layer),
            _layer_spec((FFN_HIDDEN, D_MODEL), layer),
        ],
        out_specs=row_spec(D_MODEL),
        out_shape=jax.ShapeDtypeStruct((rows, D_MODEL), F32),
        scratch_shapes=[pltpu.VMEM((tm, FFN_HIDDEN), BF16)],
        compiler_params=pltpu.CompilerParams(
            dimension_semantics=("arbitrary",), vmem_limit_bytes=VMEM_LIMIT),
        name="out_latent" if latent else "out_ctx",
    )(x2d, attn, conv, ret, mod, pw["g2"], pw["wo"], pw["wg"], pw["wu"], pw["wd"])


def _pad_heads(w, heads, width):
    lead = w.shape[:-1]
    w = w.reshape(*lead, heads, width)
    w = jnp.pad(w, [(0, 0)] * len(lead) + [(0, 0), (0, HEAD_PAD - width)])
    return w.reshape(*lead, heads * HEAD_PAD)


def _swap_rope_pairs(w):
    nope = jnp.zeros_like(w[..., :QK_NOPE])
    rope = w[..., QK_NOPE:].reshape(*w.shape[:-1], 2, 2, ROPE_PAIR)
    rope = rope[..., ::-1, :].reshape(*w.shape[:-1], QK_ROPE)
    return jnp.concatenate([nope, rope], axis=-1)


def _prep_weights(w_in, q_norm_g, kv_norm_g, w_q_up, w_kv_up, q_head_norm_g, k_head_norm_g,
                  conv_w, norm1_g, norm2_g, w_o, w_ffn_gate, w_ffn_up, w_ffn_down):
    d = DEPTH
    zeros = lambda n: jnp.zeros((d, D_MODEL, n), w_in.dtype)
    w_in_p = jnp.concatenate([w_in[..., :C_KPE], zeros(QK_NOPE), w_in[..., C_KPE:C_KPE + QK_ROPE],
                              zeros(LANES - QK_HEAD), w_in[..., C_KPE + QK_ROPE:]], axis=-1)
    wq_h = w_q_up.reshape(d, Q_LORA, MLA_HEADS, QK_HEAD)
    wq = jnp.concatenate([_pad_heads(wq_h.reshape(d, Q_LORA, -1), MLA_HEADS, QK_HEAD),
                          _pad_heads(_swap_rope_pairs(wq_h).reshape(d, Q_LORA, -1), MLA_HEADS, QK_HEAD)],
                         axis=-1)
    kvu = w_kv_up.reshape(d, KV_LORA, MLA_HEADS, QK_NOPE + V_HEAD)
    wk = _pad_heads(kvu[..., :QK_NOPE].reshape(d, KV_LORA, -1), MLA_HEADS, QK_NOPE)
    wv = kvu[..., QK_NOPE:].reshape(d, KV_LORA, MLA_W)
    lane_pad = ((0, 0), (0, HEAD_PAD - QK_HEAD))
    qhg = jnp.stack([jnp.pad(q_head_norm_g, lane_pad),
                     jnp.pad(_swap_rope_pairs(q_head_norm_g), lane_pad)], axis=1)
    return {
        "g1": norm1_g.reshape(d, 1, D_MODEL),
        "g2": norm2_g.reshape(d, 1, D_MODEL),
        "w_in": w_in_p.astype(BF16),
        "qng": q_norm_g.reshape(d, 1, Q_LORA),
        "kvng": kv_norm_g.reshape(d, 1, KV_LORA),
        "wq": wq.astype(BF16),
        "wkv": jnp.concatenate([wk, wv], axis=-1).astype(BF16),
        "qhg": jnp.pad(qhg, ((0, 0), (0, SUBLANES - 2), (0, 0))),
        "khg": jnp.pad(k_head_norm_g, lane_pad).reshape(d, 1, LANES),
        "convw": jnp.pad(conv_w, ((0, 0), (0, SUBLANES - 3), (0, 0))),
        "wo": w_o.astype(BF16),
        "wg": w_ffn_gate.astype(BF16),
        "wu": w_ffn_up.astype(BF16),
        "wd": w_ffn_down.astype(BF16),
    }


def _rope_tables(t):
    half = QK_ROPE // 2
    freqs = jnp.power(ROPE_THETA, -jnp.arange(0, half, 2, dtype=F32) / half)
    lane = jnp.arange(LANES)
    rel = lane - QK_NOPE
    in_rope = (rel >= 0) & (rel < QK_ROPE)
    freq_lane = freqs[jnp.clip(rel, 0, QK_ROPE - 1) % ROPE_PAIR]
    tok = jnp.arange(t)
    row = (tok // GRID_W).astype(F32)
    col = (tok % GRID_W).astype(F32)
    pos = jnp.where((rel < half)[None, :], row[:, None], col[:, None])
    ang = pos * freq_lane[None, :]
    first = in_rope & ((rel % half) < ROPE_PAIR)
    second = in_rope & ((rel % half) >= ROPE_PAIR)
    cos = jnp.where(in_rope[None, :], jnp.cos(ang), 1.0)
    sin = jnp.sin(ang)
    sin_lo = jnp.where(first[None, :], -sin, 0.0)
    sin_hi = jnp.where(second[None, :], sin, 0.0)
    return cos, sin_lo, sin_hi


def _decay_lanes(ret_decay_fwd, ret_decay_bwd):
    lg = jnp.stack([jax.nn.log_sigmoid(ret_decay_fwd.astype(F32)),
                    jax.nn.log_sigmoid(ret_decay_bwd.astype(F32))], axis=1)
    pad = ((0, 0), (0, SUBLANES - 2), (0, 0))
    return (jnp.pad(jnp.repeat(lg, RET_DV, axis=2), pad),
            jnp.pad(jnp.repeat(lg, RET_CHUNK, axis=2), pad))


TM = 256
TQ = 256


def kernel(x_prompt, x_sample, cache_ckv, cache_kpe, state_ret, c, c_ctx, ada_w, ada_b, norm1_g, norm2_g, w_in, q_norm_g, kv_norm_g, w_q_up, w_kv_up, q_head_norm_g, k_head_norm_g, conv_w, ret_decay_fwd, ret_decay_bwd, w_o, w_ffn_gate, w_ffn_up, w_ffn_down):
    bp, tp, _ = x_prompt.shape
    bs, ts, _ = x_sample.shape

    cond = jnp.concatenate([c_ctx[None, :], c, jnp.zeros((SUBLANES - 1 - bs, D_MODEL), F32)], axis=0)
    mods = _ada_call(cond, ada_w, ada_b).reshape(DEPTH, SUBLANES, 1, 6 * D_MODEL)

    pw = _prep_weights(w_in, q_norm_g, kv_norm_g, w_q_up, w_kv_up, q_head_norm_g, k_head_norm_g,
                       conv_w, norm1_g, norm2_g, w_o, w_ffn_gate, w_ffn_up, w_ffn_down)
    rope_tabs = _rope_tables(ts)
    lgl, lgc = _decay_lanes(ret_decay_fwd, ret_decay_bwd)
    kpe_ctx = jnp.pad(cache_kpe, ((0, 0), (0, 0), (0, 0), (QK_NOPE, LANES - QK_HEAD)))
    kc, vc = _ctx_kv_call(cache_ckv, kpe_ctx, pw["wkv"], pw["khg"])

    xp = x_prompt.reshape(bp * tp, D_MODEL)
    xs = x_sample.reshape(bs * ts, D_MODEL)
    ckv_list, kpe_list, st_list = [], [], []
    for l in range(DEPTH):
        attn, ckv, kpe, conv, rq, rk, rv, rg = _proj_call(
            xp, mods, pw, l, None, latent=False, seq=tp, tm=tp)
        r3 = lambda a: a.reshape(bp, tp, RET_W)
        ret, s_fin = _ret_call(lgl, lgc, r3(rq), r3(rk), r3(rv), r3(rg), None, l)
        xp = _out_call(xp, attn, conv, ret.reshape(bp * tp, RET_W), mods, pw, l,
                       latent=False, seq=tp, tm=TM)
        ckv_list.append(ckv.reshape(bp, tp, KV_LORA))
        kpe_list.append(kpe.reshape(bp, tp, QK_ROPE))
        st_list.append(s_fin)

        q, k, v, conv, rq, rk, rv, rg = _proj_call(
            xs, mods, pw, l, rope_tabs, latent=True, seq=ts, tm=TM)
        attn = _attn_call(q.reshape(bs, ts, QK_W), k.reshape(bs, ts, QK_W),
                          v.reshape(bs, ts, MLA_W), kc, vc, l, tq=TQ)
        r3 = lambda a: a.reshape(bs, ts, RET_W)
        (ret,) = _ret_call(lgl, lgc, r3(rq), r3(rk), r3(rv), r3(rg), state_ret, l)
        xs = _out_call(xs, attn.reshape(bs * ts, MLA_W), conv, ret.reshape(bs * ts, RET_W),
                       mods, pw, l, latent=True, seq=ts, tm=TM)

    return (xp.reshape(bp, tp, D_MODEL), xs.reshape(bs, ts, D_MODEL),
            jnp.stack(ckv_list, axis=1), jnp.stack(kpe_list, axis=1), jnp.stack(st_list, axis=1))
```

```python
import functools

import jax
import jax.numpy as jnp
from jax import lax
from jax.experimental import pallas as pl
from jax.experimental.pallas import tpu as pltpu

D_MODEL = 1024
DEPTH = 2
GRID_W = 64
MLA_HEADS = 8
Q_LORA = 256
KV_LORA = 128
QK_NOPE = 64
QK_ROPE = 32
V_HEAD = 64
QK_HEAD = QK_NOPE + QK_ROPE
MLA_W = MLA_HEADS * V_HEAD
CONV_W = 256
RET_HEADS = 4
RET_DK = 64
RET_DV = 64
RET_W = RET_HEADS * RET_DV
RET_CHUNK = 128
FFN_HIDDEN = 2816
ROPE_THETA = 10000.0
EPS = 1e-6

LANES = 128
SUBLANES = 8
MXU_N = 256
HEAD_PAD = LANES
QK_W = MLA_HEADS * HEAD_PAD
ROPE_PAIR = QK_ROPE // 4
VMEM_LIMIT = 56 * 1024 * 1024

C_QLAT = 0
C_KVLAT = C_QLAT + Q_LORA
C_KPE = C_KVLAT + KV_LORA
C_GB = C_KPE + LANES
C_GC = C_GB + CONV_W
C_XIN = C_GC + CONV_W
C_RQ = C_XIN + CONV_W
C_RK = C_RQ + RET_W
C_RV = C_RK + RET_W
C_RG = C_RV + RET_W
IN_COLS_PAD = C_RG + RET_W

BF16 = jnp.bfloat16
F32 = jnp.float32


def _dot(a, b):
    return jnp.dot(a, b, preferred_element_type=F32)


def _dot_nt(a, b):
    return lax.dot_general(a, b, (((1,), (1,)), ((), ())), preferred_element_type=F32)


def _rms(x, n=None):
    n = x.shape[-1] if n is None else n
    return x * lax.rsqrt(jnp.sum(x * x, axis=-1, keepdims=True) * (1.0 / n) + EPS)


def _silu(x):
    return x * (1.0 / (1.0 + jnp.exp(-x)))


def _const_spec(shape):
    nd = len(shape)
    return pl.BlockSpec(shape, lambda *_: (0,) * nd, pipeline_mode=pl.Buffered(1))


def _layer_spec(shape, layer):
    nd = len(shape)
    return pl.BlockSpec((None,) + tuple(shape), lambda *_: (layer,) + (0,) * nd,
                        pipeline_mode=pl.Buffered(1))


def _softmax_pv(s_list, v_list):
    m = functools.reduce(jnp.maximum, [jnp.max(s, axis=-1, keepdims=True) for s in s_list])
    l, o = None, None
    for s, v in zip(s_list, v_list):
        p = jnp.exp(s - m)
        ls = jnp.sum(p, axis=-1, keepdims=True)
        os_ = _dot(p.astype(BF16), v)
        l = ls if l is None else l + ls
        o = os_ if o is None else o + os_
    return o * (1.0 / l)


def _merge_head_pair(o0, o1):
    lane = lax.broadcasted_iota(jnp.int32, o0.shape, 1)
    return jnp.where(lane < V_HEAD, o0, o1).astype(BF16)


ADA_TN = 1536


def _ada_kernel(cond_ref, w_ref, b_ref, o_ref):
    a = _silu(cond_ref[...]).astype(BF16)
    o_ref[...] = _dot(a, w_ref[...].astype(BF16)) + b_ref[...]


def _ada_call(cond, ada_w, ada_b):
    n = 6 * D_MODEL
    return pl.pallas_call(
        _ada_kernel,
        grid=(DEPTH, n // ADA_TN),
        in_specs=[
            pl.BlockSpec((SUBLANES, D_MODEL), lambda l, j: (0, 0)),
            pl.BlockSpec((None, D_MODEL, ADA_TN), lambda l, j: (l, 0, j)),
            pl.BlockSpec((None, 1, ADA_TN), lambda l, j: (l, 0, j)),
        ],
        out_specs=pl.BlockSpec((None, SUBLANES, ADA_TN), lambda l, j: (l, 0, j)),
        out_shape=jax.ShapeDtypeStruct((DEPTH, SUBLANES, n), F32),
        compiler_params=pltpu.CompilerParams(
            dimension_semantics=("arbitrary", "arbitrary"), vmem_limit_bytes=VMEM_LIMIT),
        name="ada_mod",
    )(cond, ada_w, ada_b.reshape(DEPTH, 1, n))


def _key_slab(kn, shared, pe_ssq, g):
    ssq = jnp.sum(kn * kn, axis=-1, keepdims=True) + pe_ssq
    return ((kn * g + shared) * lax.rsqrt(ssq * (1.0 / QK_HEAD) + EPS)).astype(BF16)


def _ctx_kv_kernel(ckv_ref, kpe_ref, wkv_ref, khg_ref, k_ref, v_ref):
    kv = _dot(ckv_ref[...].astype(BF16), wkv_ref[...])
    kpe = kpe_ref[...]
    g = khg_ref[...]
    shared = kpe * g
    pe_ssq = jnp.sum(kpe * kpe, axis=-1, keepdims=True)
    for hd in range(MLA_HEADS):
        sl = slice(hd * HEAD_PAD, (hd + 1) * HEAD_PAD)
        k_ref[:, sl] = _key_slab(kv[:, sl], shared, pe_ssq, g)
    v_ref[...] = kv[:, QK_W:].astype(BF16)


def _ctx_kv_call(cache_ckv, kpe_slab, wkv, khg):
    b, _, t, _ = cache_ckv.shape
    return pl.pallas_call(
        _ctx_kv_kernel,
        grid=(DEPTH, b),
        in_specs=[
            pl.BlockSpec((None, None, t, KV_LORA), lambda l, i: (i, l, 0, 0)),
            pl.BlockSpec((None, None, t, LANES), lambda l, i: (i, l, 0, 0)),
            pl.BlockSpec((None, KV_LORA, QK_W + MLA_W), lambda l, i: (l, 0, 0)),
            pl.BlockSpec((None, 1, LANES), lambda l, i: (l, 0, 0)),
        ],
        out_specs=[
            pl.BlockSpec((None, None, t, QK_W), lambda l, i: (l, i, 0, 0)),
            pl.BlockSpec((None, None, t, MLA_W), lambda l, i: (l, i, 0, 0)),
        ],
        out_shape=[
            jax.ShapeDtypeStruct((DEPTH, b, t, QK_W), BF16),
            jax.ShapeDtypeStruct((DEPTH, b, t, MLA_W), BF16),
        ],
        compiler_params=pltpu.CompilerParams(
            dimension_semantics=("arbitrary", "arbitrary"), vmem_limit_bytes=VMEM_LIMIT),
        name="ctx_kv",
    )(cache_ckv, kpe_slab, wkv, khg)


def _proj_kernel(*refs, latent, tiles_per_seq, tm):
    it = iter(refs)
    x_ref = next(it)
    if latent:
        xp_ref, xn_ref = next(it), next(it)
    mod_ref, g1_ref, win_ref, qng_ref, kvng_ref = (next(it) for _ in range(5))
    wq_ref, wkv_ref, qhg_ref, khg_ref, cw_ref = (next(it) for _ in range(5))
    if latent:
        cos_ref, slo_ref, shi_ref = next(it), next(it), next(it)
        q_out, k_out, v_out = next(it), next(it), next(it)
    else:
        attn_out, ckv_out, kpe_out = next(it), next(it), next(it)
    conv_out, rq_out, rk_out, rv_out, rg_out = (next(it) for _ in range(5))

    shift = mod_ref[:, 0:D_MODEL]
    scale1 = 1.0 + mod_ref[:, D_MODEL:2 * D_MODEL]
    g1 = g1_ref[...]

    def norm_mod(x):
        return ((_rms(x) * g1) * scale1 + shift).astype(BF16)

    h = norm_mod(x_ref[...])

    def proj(col, width, hh=h):
        return _dot(hh, win_ref[:, col:col + width])

    q_scale = QK_HEAD ** -0.5
    qg = qhg_ref[0:1, :] * q_scale
    kg = khg_ref[...]

    def conv_inputs():
        u = proj(C_GC, CONV_W) * proj(C_XIN, CONV_W)
        if latent:
            i = pl.program_id(0)
            hh = norm_mod(jnp.concatenate([xp_ref[...], xn_ref[...]], axis=0))
            uh = proj(C_GC, CONV_W, hh) * proj(C_XIN, CONV_W, hh)
            has_prev = jnp.where(i % tiles_per_seq != 0, 1.0, 0.0)
            has_next = jnp.where(i % tiles_per_seq != tiles_per_seq - 1, 1.0, 0.0)
            u_before = uh[SUBLANES - 1:SUBLANES, :] * has_prev
            u_after = uh[SUBLANES:SUBLANES + 1, :] * has_next
        else:
            u_before = jnp.zeros((1, CONV_W), F32)
            u_after = jnp.zeros((1, CONV_W), F32)
        return u, u_before, u_after

    def conv_finish(u, u_before, u_after):
        row = lax.broadcasted_iota(jnp.int32, (tm, CONV_W), 0)
        u_m1 = jnp.where(row == 0, u_before, pltpu.roll(u, 1, axis=0))
        u_p1 = jnp.where(row == tm - 1, u_after, pltpu.roll(u, tm - 1, axis=0))
        y = u_m1 * cw_ref[0:1, :] + u * cw_ref[1:2, :] + u_p1 * cw_ref[2:3, :]
        conv_out[...] = (proj(C_GB, CONV_W) * y).astype(BF16)

    def ret_q():
        rq_out[...] = proj(C_RQ, RET_W).astype(BF16)

    def ret_k():
        rk_out[...] = (proj(C_RK, RET_W) * (RET_DK ** -0.5)).astype(BF16)

    def ret_v():
        rv_out[...] = proj(C_RV, RET_W).astype(BF16)

    def ret_g():
        rg_out[...] = proj(C_RG, RET_W)

    q_lat = proj(C_QLAT, Q_LORA)
    kv_pe = proj(C_KVLAT, KV_LORA + LANES)
    conv_in = conv_inputs()

    qn = (_rms(q_lat) * qng_ref[...]).astype(BF16)
    q = _dot(qn, wq_ref[:, 0:QK_W])
    if latent:
        q_sw = _dot(qn, wq_ref[:, QK_W:2 * QK_W])
        cos = cos_ref[...]
        sin = slo_ref[...] + shi_ref[...]
        q_tab = qg * cos
        q_tab_sw = (qhg_ref[1:2, :] * q_scale) * sin

    def query_slab(sl):
        qs = q[:, sl]
        r = lax.rsqrt(jnp.sum(qs * qs, axis=-1, keepdims=True) * (1.0 / QK_HEAD) + EPS)
        if latent:
            return ((qs * q_tab + q_sw[:, sl] * q_tab_sw) * r).astype(BF16)
        return (qs * qg * r).astype(BF16)

    ckv = _rms(kv_pe[:, 0:KV_LORA]) * kvng_ref[...]
    kpe = kv_pe[:, KV_LORA:]
    kv = _dot(ckv.astype(BF16), wkv_ref[...])
    conv_finish(*conv_in)
    ret_q()
    shared = kpe * kg
    if latent:
        shared = (shared * cos + pltpu.roll(shared, LANES - ROPE_PAIR, axis=1) * slo_ref[...]
                  + pltpu.roll(shared, ROPE_PAIR, axis=1) * shi_ref[...])
    pe_ssq = jnp.sum(kpe * kpe, axis=-1, keepdims=True)
    head_slice = lambda hd: slice(hd * HEAD_PAD, (hd + 1) * HEAD_PAD)
    fillers = {1: ret_k, 3: ret_v, 5: ret_g}

    if latent:
        for hd in range(MLA_HEADS):
            sl = head_slice(hd)
            q_out[:, sl] = query_slab(sl)
            k_out[:, sl] = _key_slab(kv[:, sl], shared, pe_ssq, kg)
            if hd in fillers:
                fillers[hd]()
        v_out[...] = kv[:, QK_W:].astype(BF16)
    else:
        ckv_out[...] = ckv
        kpe_out[...] = kpe[:, QK_NOPE:QK_HEAD]

        def scores(hd):
            sl = head_slice(hd)
            return _dot_nt(query_slab(sl), _key_slab(kv[:, sl], shared, pe_ssq, kg))

        ahead = [scores(hd) for hd in range(CTX_ATTN_LOOKAHEAD)]
        pair = []
        for hd in range(MLA_HEADS):
            hp = hd // 2
            s = ahead.pop(0)
            if hd + CTX_ATTN_LOOKAHEAD < MLA_HEADS:
                ahead.append(scores(hd + CTX_ATTN_LOOKAHEAD))
            if hd in fillers:
                fillers[hd]()
            v_pair = kv[:, QK_W + hp * LANES:QK_W + (hp + 1) * LANES].astype(BF16)
            pair.append(_softmax_pv([s], [v_pair]))
            if hd % 2 == 1:
                attn_out[:, hp * LANES:(hp + 1) * LANES] = _merge_head_pair(*pair)
                pair = []


def _proj_call(x2d, mod, pw, layer, rope_tabs, *, latent, seq, tm):
    rows = x2d.shape[0]
    tiles_per_seq = seq // tm
    n_tiles = rows // tm
    if latent:
        mod_map = lambda i: (layer, 1 + i // tiles_per_seq, 0, 0)
    else:
        assert tiles_per_seq == 1, "the context path runs its attention inside the proj tile"
        mod_map = lambda i: (layer, 0, 0, 0)
    row_spec = lambda w: pl.BlockSpec((tm, w), lambda i: (i, 0))
    in_specs = [row_spec(D_MODEL)]
    args = [x2d]
    if latent:
        blk = tm // SUBLANES
        last = rows // SUBLANES - 1
        in_specs += [
            pl.BlockSpec((SUBLANES, D_MODEL), lambda i: (jnp.maximum(i * blk - 1, 0), 0)),
            pl.BlockSpec((SUBLANES, D_MODEL), lambda i: (jnp.minimum((i + 1) * blk, last), 0)),
        ]
        args += [x2d, x2d]
    in_specs += [
        pl.BlockSpec((None, None, 1, 6 * D_MODEL), mod_map),
        _layer_spec((1, D_MODEL), layer),
        _layer_spec((D_MODEL, IN_COLS_PAD), layer),
        _layer_spec((1, Q_LORA), layer),
        _layer_spec((1, KV_LORA), layer),
        _layer_spec((Q_LORA, 2 * QK_W), layer),
        _layer_spec((KV_LORA, QK_W + MLA_W), layer),
        _layer_spec((SUBLANES, LANES), layer),
        _layer_spec((1, LANES), layer),
        _layer_spec((SUBLANES, CONV_W), layer),
    ]
    args += [mod, pw["g1"], pw["w_in"], pw["qng"], pw["kvng"], pw["wq"], pw["wkv"],
             pw["qhg"], pw["khg"], pw["convw"]]
    if latent:
        in_specs += [pl.BlockSpec((tm, LANES), lambda i: (i % tiles_per_seq, 0))] * 3
        args += list(rope_tabs)
        outs = [(QK_W, BF16), (QK_W, BF16), (MLA_W, BF16)]
    else:
        outs = [(MLA_W, BF16), (KV_LORA, F32), (QK_ROPE, F32)]
    outs += [(CONV_W, BF16), (RET_W, BF16), (RET_W, BF16), (RET_W, BF16), (RET_W, F32)]
    return pl.pallas_call(
        functools.partial(_proj_kernel, latent=latent, tiles_per_seq=tiles_per_seq, tm=tm),
        grid=(n_tiles,),
        in_specs=in_specs,
        out_specs=[row_spec(w) for w, _ in outs],
        out_shape=[jax.ShapeDtypeStruct((rows, w), dt) for w, dt in outs],
        compiler_params=pltpu.CompilerParams(
            dimension_semantics=("arbitrary",), vmem_limit_bytes=VMEM_LIMIT),
        name="proj_latent" if latent else "proj_ctx",
    )(*args)


def _attn_kernel(q_ref, k_ref, v_ref, kc_ref, vc_ref, o_ref, *, unit):
    units = [(r0, j) for r0 in range(0, q_ref.shape[0], unit) for j in range(2)]

    def scores(r0, j):
        sl = slice(j * HEAD_PAD, (j + 1) * HEAD_PAD)
        q = q_ref[r0:r0 + unit, sl]
        return [_dot_nt(q, k_ref[:, sl]), _dot_nt(q, kc_ref[:, sl])]

    ahead = [scores(*u) for u in units[:ATTN_LOOKAHEAD]]
    pair = []
    for idx, (r0, j) in enumerate(units):
        s = ahead.pop(0)
        if idx + ATTN_LOOKAHEAD < len(units):
            ahead.append(scores(*units[idx + ATTN_LOOKAHEAD]))
        pair.append(_softmax_pv(s, [v_ref[...], vc_ref[...]]))
        if j == 1:
            o_ref[r0:r0 + unit, :] = _merge_head_pair(*pair)
            pair = []


def _attn_call(q, k, v, kc, vc, layer, *, tq, unit):
    b, t, _ = q.shape
    tk, tc = k.shape[1], kc.shape[2]
    return pl.pallas_call(
        functools.partial(_attn_kernel, unit=unit),
        grid=(b, MLA_HEADS // 2, t // tq),
        in_specs=[
            pl.BlockSpec((None, tq, 2 * HEAD_PAD), lambda bi, hp, i: (bi, i, hp)),
            pl.BlockSpec((None, tk, 2 * HEAD_PAD), lambda bi, hp, i: (bi, 0, hp)),
            pl.BlockSpec((None, tk, 2 * V_HEAD), lambda bi, hp, i: (bi, 0, hp)),
            pl.BlockSpec((None, None, tc, 2 * HEAD_PAD), lambda bi, hp, i: (layer, bi, 0, hp)),
            pl.BlockSpec((None, None, tc, 2 * V_HEAD), lambda bi, hp, i: (layer, bi, 0, hp)),
        ],
        out_specs=pl.BlockSpec((None, tq, 2 * V_HEAD), lambda bi, hp, i: (bi, i, hp)),
        out_shape=jax.ShapeDtypeStruct((b, t, MLA_W), BF16),
        compiler_params=pltpu.CompilerParams(
            dimension_semantics=("arbitrary", "arbitrary", "arbitrary"),
            vmem_limit_bytes=VMEM_LIMIT),
        name="attn_latent",
    )(q, k, v, kc, vc)


def _ret_kernel(*refs, latent, n_chunks, group):
    if latent:
        lgl_ref, lgc_ref, rq_ref, rk_ref, rv_ref, rg_ref, s0_ref, out_ref, o_acc, st_ref = refs
    else:
        lgl_ref, lgc_ref, rq_ref, rk_ref, rv_ref, rg_ref, out_ref, sfin_ref, o_acc, st_ref = refs
    c = RET_CHUNK
    seqs = range(group)
    lg_f, lg_b = lgl_ref[0:1, :], lgl_ref[1:2, :]
    pos = lax.broadcasted_iota(jnp.int32, (c, RET_W), 0).astype(F32)
    qdec_f = jnp.exp(lg_f * (pos + 1.0))
    kdec_f = jnp.exp(lg_f * (c - 1.0 - pos))
    qdec_b = jnp.exp(lg_b * (c - pos))
    kdec_b = jnp.exp(lg_b * pos)
    cdec_f = jnp.exp(lg_f * float(c))
    cdec_b = jnp.exp(lg_b * float(c))
    ri = lax.broadcasted_iota(jnp.int32, (c, RET_HEADS * c), 0)
    ci = jnp.bitwise_and(lax.broadcasted_iota(jnp.int32, (c, RET_HEADS * c), 1), c - 1)
    diff = (ri - ci).astype(F32)
    dcat = (jnp.where(diff >= 0, jnp.exp(lgc_ref[0:1, :] * diff), 0.0)
            + jnp.where(diff <= 0, jnp.exp(lgc_ref[1:2, :] * (-diff)), 0.0))
    head_shift = RET_DV.bit_length() - 1
    lane_head = jnp.right_shift(lax.broadcasted_iota(jnp.int32, (c, RET_W), 1), head_shift)
    r_head = jnp.right_shift(lax.broadcasted_iota(jnp.int32, (RET_W, RET_W), 0), head_shift)
    c_head = jnp.right_shift(lax.broadcasted_iota(jnp.int32, (RET_W, RET_W), 1), head_shift)
    same_head = r_head == c_head
    head_mean = jnp.where(same_head, 1.0 / RET_DV, 0.0).astype(BF16)

    def stack_heads(a):
        zero = jnp.zeros_like(a)
        return jnp.concatenate([jnp.where(lane_head == hd, a, zero) for hd in range(RET_HEADS)], axis=0)

    def chunk(n):
        return n * c if isinstance(n, int) else pl.multiple_of(n * c, c)

    def scan(body):
        if n_chunks <= 2:
            for j in range(n_chunks):
                body(j)
        else:
            lax.fori_loop(0, n_chunks, lambda j, carry: (body(j), carry)[1], 0)

    def state_terms(qs, ks, vs, qdec, kdec):
        inc = [_dot((k.astype(F32) * kdec).T.astype(BF16), v) for k, v in zip(ks, vs)]
        inter = [_dot((q.astype(F32) * qdec).astype(BF16), st_ref[g].astype(BF16))
                 for g, q in zip(seqs, qs)]
        return inter, inc

    def state_update(inc, cdec):
        for g in seqs:
            st_ref[g] = st_ref[g] * cdec + jnp.where(same_head, inc[g], 0.0)

    def init_state(d):
        st_ref[...] = jnp.zeros_like(st_ref)
        if latent:
            for g in seqs:
                for hd in range(RET_HEADS):
                    st_ref[g, hd * RET_DK:(hd + 1) * RET_DK, hd * RET_DV:(hd + 1) * RET_DV] = s0_ref[g, d, hd]

    def emit_state(d):
        for g in seqs:
            for hd in range(RET_HEADS):
                sfin_ref[g, d, hd] = st_ref[g, hd * RET_DK:(hd + 1) * RET_DK, hd * RET_DV:(hd + 1) * RET_DV]

    def fwd(n):
        sl = pl.ds(chunk(n), c)
        qs = [rq_ref[g, sl, :] for g in seqs]
        ks = [rk_ref[g, sl, :] for g in seqs]
        vs = [rv_ref[g, sl, :] for g in seqs]
        scores = [_dot_nt(q, stack_heads(k)) for q, k in zip(qs, ks)]
        inter, inc = state_terms(qs, ks, vs, qdec_f, kdec_f)
        for g in seqs:
            o_acc[g, sl, :] = _dot((scores[g] * dcat).astype(BF16), stack_heads(vs[g])) + inter[g]
        state_update(inc, cdec_f)

    def bwd(j):
        sl = pl.ds(chunk(n_chunks - 1 - j), c)
        qs = [rq_ref[g, sl, :] for g in seqs]
        ks = [rk_ref[g, sl, :] for g in seqs]
        vs = [rv_ref[g, sl, :] for g in seqs]
        inter, inc = state_terms(qs, ks, vs, qdec_b, kdec_b)
        os_ = [o_acc[g, sl, :] + inter[g] for g in seqs]
        sq = [o * o for o in os_]
        hi = [x.astype(BF16) for x in sq]
        ms = [_dot(h, head_mean) + _dot((x - h.astype(F32)).astype(BF16), head_mean)
              for x, h in zip(sq, hi)]
        state_update(inc, cdec_b)
        for g in seqs:
            y = os_[g] * lax.rsqrt(ms[g] + EPS)
            out_ref[g, sl, :] = (y * _silu(rg_ref[g, sl, :])).astype(BF16)

    init_state(0)
    scan(fwd)
    if not latent:
        emit_state(0)
    init_state(1)
    scan(bwd)
    if not latent:
        emit_state(1)


def _ret_call(lgl, lgc, rq, rk, rv, rg, state_ret, layer, *, group):
    b, t, _ = rq.shape
    latent = state_ret is not None
    seq_spec = pl.BlockSpec((group, t, RET_W), lambda i: (i, 0, 0))
    st_block = (2, RET_HEADS, RET_DK, RET_DV)
    in_specs = [_layer_spec((SUBLANES, RET_W), layer),
                _layer_spec((SUBLANES, RET_HEADS * RET_CHUNK), layer),
                seq_spec, seq_spec, seq_spec, seq_spec]
    args = [lgl, lgc, rq, rk, rv, rg]
    out_specs = [seq_spec]
    out_shape = [jax.ShapeDtypeStruct((b, t, RET_W), BF16)]
    if latent:
        in_specs.append(pl.BlockSpec((group, None) + st_block, lambda i: (i, layer, 0, 0, 0, 0)))
        args.append(state_ret)
    else:
        out_specs.append(pl.BlockSpec((group,) + st_block, lambda i: (i, 0, 0, 0, 0)))
        out_shape.append(jax.ShapeDtypeStruct((b,) + st_block, F32))
    return pl.pallas_call(
        functools.partial(_ret_kernel, latent=latent, n_chunks=t // RET_CHUNK, group=group),
        grid=(b // group,),
        in_specs=in_specs,
        out_specs=out_specs,
        out_shape=out_shape,
        scratch_shapes=[pltpu.VMEM((group, t, RET_W), F32), pltpu.VMEM((group, RET_W, RET_W), F32)],
        compiler_params=pltpu.CompilerParams(
            dimension_semantics=("arbitrary",), vmem_limit_bytes=VMEM_LIMIT),
        name="ret_latent" if latent else "ret_ctx",
    )(*args)


def _out_kernel(x_ref, attn_ref, conv_ref, ret_ref, mod_ref, g2_ref, wo_ref, wg_ref, wu_ref,
                wd_ref, o_ref, act_ref):
    mix = (_dot(attn_ref[...], wo_ref[0:MLA_W, :])
           + _dot(conv_ref[...], wo_ref[MLA_W:MLA_W + CONV_W, :])
           + _dot(ret_ref[...], wo_ref[MLA_W + CONV_W:, :]))
    x1 = x_ref[...] + mod_ref[:, 2 * D_MODEL:3 * D_MODEL] * mix
    h = ((_rms(x1) * g2_ref[...]) * (1.0 + mod_ref[:, 4 * D_MODEL:5 * D_MODEL])
         + mod_ref[:, 3 * D_MODEL:4 * D_MODEL]).astype(BF16)
    for j in range(FFN_HIDDEN // MXU_N):
        sl = slice(j * MXU_N, (j + 1) * MXU_N)
        gate = _dot(h, wg_ref[:, sl])
        up = _dot(h, wu_ref[:, sl])
        act_ref[:, sl] = (_silu(gate) * up).astype(BF16)
    ffn = _dot(act_ref[...], wd_ref[...])
    o_ref[...] = x1 + mod_ref[:, 5 * D_MODEL:6 * D_MODEL] * ffn


def _out_call(x2d, attn, conv, ret, mod, pw, layer, *, latent, seq, tm):
    rows = x2d.shape[0]
    tiles_per_seq = seq // tm
    if latent:
        mod_map = lambda i: (layer, 1 + i // tiles_per_seq, 0, 0)
    else:
        mod_map = lambda i: (layer, 0, 0, 0)
    row_spec = lambda w: pl.BlockSpec((tm, w), lambda i: (i, 0))
    return pl.pallas_call(
        _out_kernel,
        grid=(rows // tm,),
        in_specs=[
            row_spec(D_MODEL), row_spec(MLA_W), row_spec(CONV_W), row_spec(RET_W),
            pl.BlockSpec((None, None, 1, 6 * D_MODEL), mod_map),
            _layer_spec((1, D_MODEL), layer),
            _layer_spec((D_MODEL, D_MODEL), layer),
            _layer_spec((D_MODEL, FFN_HIDDEN), layer),
            _layer_spec((D_MODEL, FFN_HIDDEN), layer),
            _layer_spec((FFN_HIDDEN, D_MODEL), layer),
        ],
        out_specs=row_spec(D_MODEL),
        out_shape=jax.ShapeDtypeStruct((rows, D_MODEL), F32),
        scratch_shapes=[pltpu.VMEM((tm, FFN_HIDDEN), BF16)],
        compiler_params=pltpu.CompilerParams(
            dimension_semantics=("arbitrary",), vmem_limit_bytes=VMEM_LIMIT),
        name="out_latent" if latent else "out_ctx",
    )(x2d, attn, conv, ret, mod, pw["g2"], pw["wo"], pw["wg"], pw["wu"], pw["wd"])


def _pad_heads(w, heads, width):
    lead = w.shape[:-1]
    w = w.reshape(*lead, heads, width)
    w = jnp.pad(w, [(0, 0)] * len(lead) + [(0, 0), (0, HEAD_PAD - width)])
    return w.reshape(*lead, heads * HEAD_PAD)


def _swap_rope_pairs(w):
    nope = jnp.zeros_like(w[..., :QK_NOPE])
    rope = w[..., QK_NOPE:].reshape(*w.shape[:-1], 2, 2, ROPE_PAIR)
    rope = rope[..., ::-1, :].reshape(*w.shape[:-1], QK_ROPE)
    return jnp.concatenate([nope, rope], axis=-1)


def _prep_weights(w_in, q_norm_g, kv_norm_g, w_q_up, w_kv_up, q_head_norm_g, k_head_norm_g,
                  conv_w, norm1_g, norm2_g, w_o, w_ffn_gate, w_ffn_up, w_ffn_down):
    d = DEPTH
    zeros = lambda n: jnp.zeros((d, D_MODEL, n), w_in.dtype)
    w_in_p = jnp.concatenate([w_in[..., :C_KPE], zeros(QK_NOPE), w_in[..., C_KPE:C_KPE + QK_ROPE],
                              zeros(LANES - QK_HEAD), w_in[..., C_KPE + QK_ROPE:]], axis=-1)
    wq_h = w_q_up.reshape(d, Q_LORA, MLA_HEADS, QK_HEAD)
    wq = jnp.concatenate([_pad_heads(wq_h.reshape(d, Q_LORA, -1), MLA_HEADS, QK_HEAD),
                          _pad_heads(_swap_rope_pairs(wq_h).reshape(d, Q_LORA, -1), MLA_HEADS, QK_HEAD)],
                         axis=-1)
    kvu = w_kv_up.reshape(d, KV_LORA, MLA_HEADS, QK_NOPE + V_HEAD)
    wk = _pad_heads(kvu[..., :QK_NOPE].reshape(d, KV_LORA, -1), MLA_HEADS, QK_NOPE)
    wv = kvu[..., QK_NOPE:].reshape(d, KV_LORA, MLA_W)
    lane_pad = ((0, 0), (0, HEAD_PAD - QK_HEAD))
    qhg = jnp.stack([jnp.pad(q_head_norm_g, lane_pad),
                     jnp.pad(_swap_rope_pairs(q_head_norm_g), lane_pad)], axis=1)
    return {
        "g1": norm1_g.reshape(d, 1, D_MODEL),
        "g2": norm2_g.reshape(d, 1, D_MODEL),
        "w_in": w_in_p.astype(BF16),
        "qng": q_norm_g.reshape(d, 1, Q_LORA),
        "kvng": kv_norm_g.reshape(d, 1, KV_LORA),
        "wq": wq.astype(BF16),
        "wkv": jnp.concatenate([wk, wv], axis=-1).astype(BF16),
        "qhg": jnp.pad(qhg, ((0, 0), (0, SUBLANES - 2), (0, 0))),
        "khg": jnp.pad(k_head_norm_g, lane_pad).reshape(d, 1, LANES),
        "convw": jnp.pad(conv_w, ((0, 0), (0, SUBLANES - 3), (0, 0))),
        "wo": w_o.astype(BF16),
        "wg": w_ffn_gate.astype(BF16),
        "wu": w_ffn_up.astype(BF16),
        "wd": w_ffn_down.astype(BF16),
    }


def _rope_tables(t):
    half = QK_ROPE // 2
    freqs = jnp.power(ROPE_THETA, -jnp.arange(0, half, 2, dtype=F32) / half)
    lane = jnp.arange(LANES)
    rel = lane - QK_NOPE
    in_rope = (rel >= 0) & (rel < QK_ROPE)
    freq_lane = freqs[jnp.clip(rel, 0, QK_ROPE - 1) % ROPE_PAIR]
    tok = jnp.arange(t)
    row = (tok // GRID_W).astype(F32)
    col = (tok % GRID_W).astype(F32)
    pos = jnp.where((rel < half)[None, :], row[:, None], col[:, None])
    ang = pos * freq_lane[None, :]
    first = in_rope & ((rel % half) < ROPE_PAIR)
    second = in_rope & ((rel % half) >= ROPE_PAIR)
    cos = jnp.where(in_rope[None, :], jnp.cos(ang), 1.0)
    sin = jnp.sin(ang)
    sin_lo = jnp.where(first[None, :], -sin, 0.0)
    sin_hi = jnp.where(second[None, :], sin, 0.0)
    return cos, sin_lo, sin_hi


def _decay_lanes(ret_decay_fwd, ret_decay_bwd):
    lg = jnp.stack([jax.nn.log_sigmoid(ret_decay_fwd.astype(F32)),
                    jax.nn.log_sigmoid(ret_decay_bwd.astype(F32))], axis=1)
    pad = ((0, 0), (0, SUBLANES - 2), (0, 0))
    return (jnp.pad(jnp.repeat(lg, RET_DV, axis=2), pad),
            jnp.pad(jnp.repeat(lg, RET_CHUNK, axis=2), pad))


TM = 256
TM_OUT = 512
RET_GROUP = 4
TQ = 1024
TQ_UNIT = 128
CTX_ATTN_LOOKAHEAD = 2
ATTN_LOOKAHEAD = 1


def kernel(x_prompt, x_sample, cache_ckv, cache_kpe, state_ret, c, c_ctx, ada_w, ada_b, norm1_g, norm2_g, w_in, q_norm_g, kv_norm_g, w_q_up, w_kv_up, q_head_norm_g, k_head_norm_g, conv_w, ret_decay_fwd, ret_decay_bwd, w_o, w_ffn_gate, w_ffn_up, w_ffn_down):
    bp, tp, _ = x_prompt.shape
    bs, ts, _ = x_sample.shape

    cond = jnp.concatenate([c_ctx[None, :], c, jnp.zeros((SUBLANES - 1 - bs, D_MODEL), F32)], axis=0)
    mods = _ada_call(cond, ada_w, ada_b).reshape(DEPTH, SUBLANES, 1, 6 * D_MODEL)

    pw = _prep_weights(w_in, q_norm_g, kv_norm_g, w_q_up, w_kv_up, q_head_norm_g, k_head_norm_g,
                       conv_w, norm1_g, norm2_g, w_o, w_ffn_gate, w_ffn_up, w_ffn_down)
    rope_tabs = _rope_tables(ts)
    lgl, lgc = _decay_lanes(ret_decay_fwd, ret_decay_bwd)
    kpe_ctx = jnp.pad(cache_kpe, ((0, 0), (0, 0), (0, 0), (QK_NOPE, LANES - QK_HEAD)))
    kc, vc = _ctx_kv_call(cache_ckv, kpe_ctx, pw["wkv"], pw["khg"])

    xp = x_prompt.reshape(bp * tp, D_MODEL)
    xs = x_sample.reshape(bs * ts, D_MODEL)
    ckv_list, kpe_list, st_list = [], [], []
    for l in range(DEPTH):
        attn, ckv, kpe, conv, rq, rk, rv, rg = _proj_call(
            xp, mods, pw, l, None, latent=False, seq=tp, tm=tp)
        r3 = lambda a: a.reshape(bp, tp, RET_W)
        ret, s_fin = _ret_call(lgl, lgc, r3(rq), r3(rk), r3(rv), r3(rg), None, l, group=RET_GROUP)
        xp = _out_call(xp, attn, conv, ret.reshape(bp * tp, RET_W), mods, pw, l,
                       latent=False, seq=tp, tm=TM_OUT)
        ckv_list.append(ckv.reshape(bp, tp, KV_LORA))
        kpe_list.append(kpe.reshape(bp, tp, QK_ROPE))
        st_list.append(s_fin)

        q, k, v, conv, rq, rk, rv, rg = _proj_call(
            xs, mods, pw, l, rope_tabs, latent=True, seq=ts, tm=TM)
        attn = _attn_call(q.reshape(bs, ts, QK_W), k.reshape(bs, ts, QK_W),
                          v.reshape(bs, ts, MLA_W), kc, vc, l, tq=TQ, unit=TQ_UNIT)
        r3 = lambda a: a.reshape(bs, ts, RET_W)
        (ret,) = _ret_call(lgl, lgc, r3(rq), r3(rk), r3(rv), r3(rg), state_ret, l, group=RET_GROUP)
        xs = _out_call(xs, attn.reshape(bs * ts, MLA_W), conv, ret.reshape(bs * ts, RET_W),
                       mods, pw, l, latent=True, seq=ts, tm=TM_OUT)

    return (xp.reshape(bp, tp, D_MODEL), xs.reshape(bs, ts, D_MODEL),
            jnp.stack(ckv_list, axis=1), jnp.stack(kpe_list, axis=1), jnp.stack(st_list, axis=1))
```

```python
import functools

import jax
import jax.numpy as jnp
from jax import lax
from jax.experimental import pallas as pl
from jax.experimental.pallas import tpu as pltpu

D_MODEL = 1024
DEPTH = 2
GRID_W = 64
MLA_HEADS = 8
Q_LORA = 256
KV_LORA = 128
QK_NOPE = 64
QK_ROPE = 32
V_HEAD = 64
QK_HEAD = QK_NOPE + QK_ROPE
MLA_W = MLA_HEADS * V_HEAD
CONV_W = 256
RET_HEADS = 4
RET_DK = 64
RET_DV = 64
RET_W = RET_HEADS * RET_DV
RET_CHUNK = 128
FFN_HIDDEN = 2816
ROPE_THETA = 10000.0
EPS = 1e-6
LOG2_E = 1.4426950408889634

LANES = 128
SUBLANES = 8
MXU_N = 256
HEAD_PAD = LANES
QK_W = MLA_HEADS * HEAD_PAD
ROPE_PAIR = QK_ROPE // 4
VMEM_LIMIT = 56 * 1024 * 1024

C_QLAT = 0
C_KVLAT = C_QLAT + Q_LORA
C_KPE = C_KVLAT + KV_LORA
C_GB = C_KPE + LANES
C_GC = C_GB + CONV_W
C_XIN = C_GC + CONV_W
C_RQ = C_XIN + CONV_W
C_RK = C_RQ + RET_W
C_RV = C_RK + RET_W
C_RG = C_RV + RET_W
IN_COLS_PAD = C_RG + RET_W

BF16 = jnp.bfloat16
F32 = jnp.float32


def _dot(a, b):
    return jnp.dot(a, b, preferred_element_type=F32)


def _dot_nt(a, b):
    return lax.dot_general(a, b, (((1,), (1,)), ((), ())), preferred_element_type=F32)


def _rms(x, n=None):
    n = x.shape[-1] if n is None else n
    return x * lax.rsqrt(jnp.sum(x * x, axis=-1, keepdims=True) * (1.0 / n) + EPS)


def _silu(x):
    return x * (1.0 / (1.0 + jnp.exp(-x)))


def _const_spec(shape):
    nd = len(shape)
    return pl.BlockSpec(shape, lambda *_: (0,) * nd, pipeline_mode=pl.Buffered(1))


def _layer_spec(shape, layer):
    nd = len(shape)
    return pl.BlockSpec((None,) + tuple(shape), lambda *_: (layer,) + (0,) * nd,
                        pipeline_mode=pl.Buffered(1))


def _softmax_pv(s_list, v_list):
    m = functools.reduce(jnp.maximum, [jnp.max(s, axis=-1, keepdims=True) for s in s_list])
    l, o = None, None
    for s, v in zip(s_list, v_list):
        p = jnp.exp2(s - m)
        ls = jnp.sum(p, axis=-1, keepdims=True)
        os_ = _dot(p.astype(BF16), v)
        l = ls if l is None else l + ls
        o = os_ if o is None else o + os_
    return o * (1.0 / l)


def _merge_head_pair(o0, o1):
    lane = lax.broadcasted_iota(jnp.int32, o0.shape, 1)
    return jnp.where(lane < V_HEAD, o0, o1).astype(BF16)


ADA_TN = 1536


def _ada_kernel(cond_ref, w_ref, b_ref, o_ref):
    a = _silu(cond_ref[...]).astype(BF16)
    o_ref[...] = _dot(a, w_ref[...].astype(BF16)) + b_ref[...]


def _ada_call(cond, ada_w, ada_b):
    n = 6 * D_MODEL
    return pl.pallas_call(
        _ada_kernel,
        grid=(DEPTH, n // ADA_TN),
        in_specs=[
            pl.BlockSpec((SUBLANES, D_MODEL), lambda l, j: (0, 0)),
            pl.BlockSpec((None, D_MODEL, ADA_TN), lambda l, j: (l, 0, j)),
            pl.BlockSpec((None, 1, ADA_TN), lambda l, j: (l, 0, j)),
        ],
        out_specs=pl.BlockSpec((None, SUBLANES, ADA_TN), lambda l, j: (l, 0, j)),
        out_shape=jax.ShapeDtypeStruct((DEPTH, SUBLANES, n), F32),
        compiler_params=pltpu.CompilerParams(
            dimension_semantics=("arbitrary", "arbitrary"), vmem_limit_bytes=VMEM_LIMIT),
        name="ada_mod",
    )(cond, ada_w, ada_b.reshape(DEPTH, 1, n))


def _key_slab(kn, shared, pe_ssq, g):
    ssq = jnp.sum(kn * kn, axis=-1, keepdims=True) + pe_ssq
    return ((kn * g + shared) * lax.rsqrt(ssq * (1.0 / QK_HEAD) + EPS)).astype(BF16)


def _ctx_kv_kernel(ckv_ref, kpe_ref, wkv_ref, khg_ref, k_ref, v_ref):
    kv = _dot(ckv_ref[...].astype(BF16), wkv_ref[...])
    kpe = kpe_ref[...]
    g = khg_ref[...]
    shared = kpe * g
    pe_ssq = jnp.sum(kpe * kpe, axis=-1, keepdims=True)
    for hd in range(MLA_HEADS):
        sl = slice(hd * HEAD_PAD, (hd + 1) * HEAD_PAD)
        k_ref[:, sl] = _key_slab(kv[:, sl], shared, pe_ssq, g)
    v_ref[...] = kv[:, QK_W:].astype(BF16)


def _ctx_kv_call(cache_ckv, kpe_slab, wkv, khg):
    b, _, t, _ = cache_ckv.shape
    return pl.pallas_call(
        _ctx_kv_kernel,
        grid=(DEPTH, b),
        in_specs=[
            pl.BlockSpec((None, None, t, KV_LORA), lambda l, i: (i, l, 0, 0)),
            pl.BlockSpec((None, None, t, LANES), lambda l, i: (i, l, 0, 0)),
            pl.BlockSpec((None, KV_LORA, QK_W + MLA_W), lambda l, i: (l, 0, 0)),
            pl.BlockSpec((None, 1, LANES), lambda l, i: (l, 0, 0)),
        ],
        out_specs=[
            pl.BlockSpec((None, None, t, QK_W), lambda l, i: (l, i, 0, 0)),
            pl.BlockSpec((None, None, t, MLA_W), lambda l, i: (l, i, 0, 0)),
        ],
        out_shape=[
            jax.ShapeDtypeStruct((DEPTH, b, t, QK_W), BF16),
            jax.ShapeDtypeStruct((DEPTH, b, t, MLA_W), BF16),
        ],
        compiler_params=pltpu.CompilerParams(
            dimension_semantics=("arbitrary", "arbitrary"), vmem_limit_bytes=VMEM_LIMIT),
        name="ctx_kv",
    )(cache_ckv, kpe_slab, wkv, khg)


def _proj_kernel(*refs, latent, tiles_per_seq, tm):
    it = iter(refs)
    x_ref = next(it)
    if latent:
        xp_ref, xn_ref = next(it), next(it)
    mod_ref, g1_ref, win_ref, qng_ref, kvng_ref = (next(it) for _ in range(5))
    wq_ref, wkv_ref, qhg_ref, khg_ref, cw_ref = (next(it) for _ in range(5))
    if latent:
        cos_ref, slo_ref, shi_ref = next(it), next(it), next(it)
        q_out, k_out, v_out = next(it), next(it), next(it)
    else:
        attn_out, ckv_out, kpe_out = next(it), next(it), next(it)
    conv_out, rq_out, rk_out, rv_out, rg_out = (next(it) for _ in range(5))

    shift = mod_ref[:, 0:D_MODEL]
    scale1 = 1.0 + mod_ref[:, D_MODEL:2 * D_MODEL]
    g1 = g1_ref[...]

    def norm_mod(x):
        return ((_rms(x) * g1) * scale1 + shift).astype(BF16)

    q_scale = QK_HEAD ** -0.5 * LOG2_E
    qg = qhg_ref[0:1, :] * q_scale
    kg = khg_ref[...]
    head_slice = lambda hd: slice(hd * HEAD_PAD, (hd + 1) * HEAD_PAD)
    n_sub = tm // PROJ_SUB
    subs = [slice(t * PROJ_SUB, (t + 1) * PROJ_SUB) for t in range(n_sub)]
    state = [dict() for _ in subs]

    def projection_stage(t):
        rows, st = subs[t], state[t]

        def proj(col, width):
            return _dot(st["h"], win_ref[:, col:col + width])

        def latents():
            st["h"] = norm_mod(x_ref[rows, :])
            st["q_lat"] = proj(C_QLAT, Q_LORA)
            st["kv_pe"] = proj(C_KVLAT, KV_LORA + LANES)

        def conv_inputs():
            st["u"] = proj(C_GC, CONV_W) * proj(C_XIN, CONV_W)
            if latent and t == 0:
                hh = norm_mod(jnp.concatenate([xp_ref[...], xn_ref[...]], axis=0))
                halo = (_dot(hh, win_ref[:, C_GC:C_GC + CONV_W])
                        * _dot(hh, win_ref[:, C_XIN:C_XIN + CONV_W]))
                i = pl.program_id(0)
                has_prev = jnp.where(i % tiles_per_seq != 0, 1.0, 0.0)
                has_next = jnp.where(i % tiles_per_seq != tiles_per_seq - 1, 1.0, 0.0)
                state[0]["u_before"] = halo[SUBLANES - 1:SUBLANES, :] * has_prev
                state[-1]["u_after"] = halo[SUBLANES:SUBLANES + 1, :] * has_next

        def queries():
            qn = (_rms(st["q_lat"]) * qng_ref[...]).astype(BF16)
            st["q"] = _dot(qn, wq_ref[:, 0:QK_W])
            if latent:
                st["q_sw"] = _dot(qn, wq_ref[:, QK_W:2 * QK_W])
                cos = cos_ref[rows, :]
                st["q_tab"] = qg * cos
                st["q_tab_sw"] = (qhg_ref[1:2, :] * q_scale) * (slo_ref[rows, :] + shi_ref[rows, :])

        def keys_values():
            ckv = _rms(st["kv_pe"][:, 0:KV_LORA]) * kvng_ref[...]
            kpe = st["kv_pe"][:, KV_LORA:]
            st["kv"] = _dot(ckv.astype(BF16), wkv_ref[...])
            shared = kpe * kg
            if latent:
                shared = (shared * cos_ref[rows, :]
                          + pltpu.roll(shared, LANES - ROPE_PAIR, axis=1) * slo_ref[rows, :]
                          + pltpu.roll(shared, ROPE_PAIR, axis=1) * shi_ref[rows, :])
            else:
                ckv_out[rows, :] = ckv
                kpe_out[rows, :] = kpe[:, QK_NOPE:QK_HEAD]
            st["shared"] = shared
            st["pe_ssq"] = jnp.sum(kpe * kpe, axis=-1, keepdims=True)

        def conv_gate():
            st["gb"] = proj(C_GB, CONV_W)

        def ret_q():
            rq_out[rows, :] = proj(C_RQ, RET_W).astype(BF16)

        def ret_k():
            rk_out[rows, :] = (proj(C_RK, RET_W) * (RET_DK ** -0.5)).astype(BF16)

        def ret_v():
            rv_out[rows, :] = proj(C_RV, RET_W).astype(BF16)

        def ret_g():
            rg_out[rows, :] = proj(C_RG, RET_W)

        return [latents, conv_inputs, queries, keys_values, conv_gate, ret_q, ret_k, ret_v, ret_g]

    def conv_finish(t):
        st, u = state[t], state[t]["u"]
        zero = jnp.zeros((1, CONV_W), F32)
        if latent:
            u_before = st["u_before"] if t == 0 else state[t - 1]["u"][PROJ_SUB - 1:PROJ_SUB, :]
            u_after = st["u_after"] if t == n_sub - 1 else state[t + 1]["u"][0:1, :]
        else:
            u_before, u_after = zero, zero
        row = lax.broadcasted_iota(jnp.int32, (PROJ_SUB, CONV_W), 0)
        u_m1 = jnp.where(row == 0, u_before, pltpu.roll(u, 1, axis=0))
        u_p1 = jnp.where(row == PROJ_SUB - 1, u_after, pltpu.roll(u, PROJ_SUB - 1, axis=0))
        y = u_m1 * cw_ref[0:1, :] + u * cw_ref[1:2, :] + u_p1 * cw_ref[2:3, :]
        conv_out[subs[t], :] = (st["gb"] * y).astype(BF16)

    def head_stage(t):
        rows, st = subs[t], state[t]

        def query_slab(sl):
            qs = st["q"][:, sl]
            r = lax.rsqrt(jnp.sum(qs * qs, axis=-1, keepdims=True) * (1.0 / QK_HEAD) + EPS)
            if latent:
                return ((qs * st["q_tab"] + st["q_sw"][:, sl] * st["q_tab_sw"]) * r).astype(BF16)
            return (qs * qg * r).astype(BF16)

        def key_slab(sl):
            return _key_slab(st["kv"][:, sl], st["shared"], st["pe_ssq"], kg)

        if latent:
            def store_head(hd):
                sl = head_slice(hd)
                q_out[rows, sl] = query_slab(sl)
                k_out[rows, sl] = key_slab(sl)
                if hd == MLA_HEADS - 1:
                    v_out[rows, :] = st["kv"][:, QK_W:].astype(BF16)
            return [functools.partial(store_head, hd) for hd in range(MLA_HEADS)]

        ahead, pair = [], []

        def scores(hd):
            sl = head_slice(hd)
            return _dot_nt(query_slab(sl), key_slab(sl))

        def attend(hd):
            if hd == 0:
                ahead.extend(scores(j) for j in range(CTX_ATTN_LOOKAHEAD))
            s = ahead.pop(0)
            if hd + CTX_ATTN_LOOKAHEAD < MLA_HEADS:
                ahead.append(scores(hd + CTX_ATTN_LOOKAHEAD))
            hp = hd // 2
            v_pair = st["kv"][:, QK_W + hp * LANES:QK_W + (hp + 1) * LANES].astype(BF16)
            pair.append(_softmax_pv([s], [v_pair]))
            if hd % 2 == 1:
                attn_out[rows, hp * LANES:(hp + 1) * LANES] = _merge_head_pair(*pair)
                pair.clear()

        return [functools.partial(attend, hd) for hd in range(MLA_HEADS)]

    for piece in projection_stage(0):
        piece()
    for t in range(n_sub):
        heads = head_stage(t)
        nxt = projection_stage(t + 1) if t + 1 < n_sub else []
        for i in range(max(len(heads), len(nxt))):
            if i < len(nxt):
                nxt[i]()
            if i < len(heads):
                heads[i]()
            if i == PROJ_CONV_SLOT and t > 0:
                conv_finish(t - 1)
    conv_finish(n_sub - 1)


def _proj_call(x2d, mod, pw, layer, rope_tabs, *, latent, seq, tm):
    rows = x2d.shape[0]
    tiles_per_seq = seq // tm
    n_tiles = rows // tm
    if latent:
        mod_map = lambda i: (layer, 1 + i // tiles_per_seq, 0, 0)
    else:
        assert seq == PROJ_SUB, "the context path runs one sequence's attention per sub-tile"
        mod_map = lambda i: (layer, 0, 0, 0)
    row_spec = lambda w: pl.BlockSpec((tm, w), lambda i: (i, 0))
    in_specs = [row_spec(D_MODEL)]
    args = [x2d]
    if latent:
        blk = tm // SUBLANES
        last = rows // SUBLANES - 1
        in_specs += [
            pl.BlockSpec((SUBLANES, D_MODEL), lambda i: (jnp.maximum(i * blk - 1, 0), 0)),
            pl.BlockSpec((SUBLANES, D_MODEL), lambda i: (jnp.minimum((i + 1) * blk, last), 0)),
        ]
        args += [x2d, x2d]
    in_specs += [
        pl.BlockSpec((None, None, 1, 6 * D_MODEL), mod_map),
        _layer_spec((1, D_MODEL), layer),
        _layer_spec((D_MODEL, IN_COLS_PAD), layer),
        _layer_spec((1, Q_LORA), layer),
        _layer_spec((1, KV_LORA), layer),
        _layer_spec((Q_LORA, 2 * QK_W), layer),
        _layer_spec((KV_LORA, QK_W + MLA_W), layer),
        _layer_spec((SUBLANES, LANES), layer),
        _layer_spec((1, LANES), layer),
        _layer_spec((SUBLANES, CONV_W), layer),
    ]
    args += [mod, pw["g1"], pw["w_in"], pw["qng"], pw["kvng"], pw["wq"], pw["wkv"],
             pw["qhg"], pw["khg"], pw["convw"]]
    if latent:
        in_specs += [pl.BlockSpec((tm, LANES), lambda i: (i % tiles_per_seq, 0))] * 3
        args += list(rope_tabs)
        outs = [(QK_W, BF16), (QK_W, BF16), (MLA_W, BF16)]
    else:
        outs = [(MLA_W, BF16), (KV_LORA, F32), (QK_ROPE, F32)]
    outs += [(CONV_W, BF16), (RET_W, BF16), (RET_W, BF16), (RET_W, BF16), (RET_W, F32)]
    return pl.pallas_call(
        functools.partial(_proj_kernel, latent=latent, tiles_per_seq=tiles_per_seq, tm=tm),
        grid=(n_tiles,),
        in_specs=in_specs,
        out_specs=[row_spec(w) for w, _ in outs],
        out_shape=[jax.ShapeDtypeStruct((rows, w), dt) for w, dt in outs],
        compiler_params=pltpu.CompilerParams(
            dimension_semantics=("arbitrary",), vmem_limit_bytes=VMEM_LIMIT),
        name="proj_latent" if latent else "proj_ctx",
    )(*args)


def _attn_kernel(q_ref, k_ref, v_ref, kc_ref, vc_ref, o_ref, *, unit):
    units = [(r0, j) for r0 in range(0, q_ref.shape[0], unit) for j in range(2)]

    def scores(r0, j):
        sl = slice(j * HEAD_PAD, (j + 1) * HEAD_PAD)
        q = q_ref[r0:r0 + unit, sl]
        return [_dot_nt(q, k_ref[:, sl]), _dot_nt(q, kc_ref[:, sl])]

    ahead = [scores(*u) for u in units[:ATTN_LOOKAHEAD]]
    pair = []
    for idx, (r0, j) in enumerate(units):
        s = ahead.pop(0)
        if idx + ATTN_LOOKAHEAD < len(units):
            ahead.append(scores(*units[idx + ATTN_LOOKAHEAD]))
        pair.append(_softmax_pv(s, [v_ref[...], vc_ref[...]]))
        if j == 1:
            o_ref[r0:r0 + unit, :] = _merge_head_pair(*pair)
            pair = []


def _attn_call(q, k, v, kc, vc, layer, *, tq, unit):
    b, t, _ = q.shape
    tk, tc = k.shape[1], kc.shape[2]
    return pl.pallas_call(
        functools.partial(_attn_kernel, unit=unit),
        grid=(b, MLA_HEADS // 2, t // tq),
        in_specs=[
            pl.BlockSpec((None, tq, 2 * HEAD_PAD), lambda bi, hp, i: (bi, i, hp)),
            pl.BlockSpec((None, tk, 2 * HEAD_PAD), lambda bi, hp, i: (bi, 0, hp)),
            pl.BlockSpec((None, tk, 2 * V_HEAD), lambda bi, hp, i: (bi, 0, hp)),
            pl.BlockSpec((None, None, tc, 2 * HEAD_PAD), lambda bi, hp, i: (layer, bi, 0, hp)),
            pl.BlockSpec((None, None, tc, 2 * V_HEAD), lambda bi, hp, i: (layer, bi, 0, hp)),
        ],
        out_specs=pl.BlockSpec((None, tq, 2 * V_HEAD), lambda bi, hp, i: (bi, i, hp)),
        out_shape=jax.ShapeDtypeStruct((b, t, MLA_W), BF16),
        compiler_params=pltpu.CompilerParams(
            dimension_semantics=("arbitrary", "arbitrary", "arbitrary"),
            vmem_limit_bytes=VMEM_LIMIT),
        name="attn_latent",
    )(q, k, v, kc, vc)


def _ret_kernel(*refs, latent, n_chunks, group):
    if latent:
        lgl_ref, lgc_ref, rq_ref, rk_ref, rv_ref, rg_ref, s0_ref, out_ref, o_acc, st_ref = refs
    else:
        lgl_ref, lgc_ref, rq_ref, rk_ref, rv_ref, rg_ref, out_ref, sfin_ref, o_acc, st_ref = refs
    c = RET_CHUNK
    seqs = range(group)
    lg_f, lg_b = lgl_ref[0:1, :], lgl_ref[1:2, :]
    pos = lax.broadcasted_iota(jnp.int32, (c, RET_W), 0).astype(F32)
    qdec_f = jnp.exp(lg_f * (pos + 1.0))
    kdec_f = jnp.exp(lg_f * (c - 1.0 - pos))
    qdec_b = jnp.exp(lg_b * (c - pos))
    kdec_b = jnp.exp(lg_b * pos)
    cdec_f = jnp.exp(lg_f * float(c))
    cdec_b = jnp.exp(lg_b * float(c))
    ri = lax.broadcasted_iota(jnp.int32, (c, RET_HEADS * c), 0)
    ci = jnp.bitwise_and(lax.broadcasted_iota(jnp.int32, (c, RET_HEADS * c), 1), c - 1)
    diff = (ri - ci).astype(F32)
    dcat = (jnp.where(diff >= 0, jnp.exp(lgc_ref[0:1, :] * diff), 0.0)
            + jnp.where(diff <= 0, jnp.exp(lgc_ref[1:2, :] * (-diff)), 0.0))
    head_shift = RET_DV.bit_length() - 1
    lane_head = jnp.right_shift(lax.broadcasted_iota(jnp.int32, (c, RET_W), 1), head_shift)
    r_head = jnp.right_shift(lax.broadcasted_iota(jnp.int32, (RET_W, RET_W), 0), head_shift)
    c_head = jnp.right_shift(lax.broadcasted_iota(jnp.int32, (RET_W, RET_W), 1), head_shift)
    same_head = r_head == c_head
    head_mean = jnp.where(same_head, 1.0 / RET_DV, 0.0).astype(BF16)

    def stack_heads(a):
        zero = jnp.zeros_like(a)
        return jnp.concatenate([jnp.where(lane_head == hd, a, zero) for hd in range(RET_HEADS)], axis=0)

    def chunk(n):
        return n * c if isinstance(n, int) else pl.multiple_of(n * c, c)

    def scan(body):
        if n_chunks <= 2:
            for j in range(n_chunks):
                body(j)
        else:
            lax.fori_loop(0, n_chunks, lambda j, carry: (body(j), carry)[1], 0)

    def state_terms(qs, ks, vs, qdec, kdec):
        inc = [_dot((k.astype(F32) * kdec).T.astype(BF16), v) for k, v in zip(ks, vs)]
        inter = [_dot((q.astype(F32) * qdec).astype(BF16), st_ref[g].astype(BF16))
                 for g, q in zip(seqs, qs)]
        return inter, inc

    def state_update(inc, cdec):
        for g in seqs:
            st_ref[g] = st_ref[g] * cdec + jnp.where(same_head, inc[g], 0.0)

    def init_state(d):
        st_ref[...] = jnp.zeros_like(st_ref)
        if latent:
            for g in seqs:
                for hd in range(RET_HEADS):
                    st_ref[g, hd * RET_DK:(hd + 1) * RET_DK, hd * RET_DV:(hd + 1) * RET_DV] = s0_ref[g, d, hd]

    def emit_state(d):
        for g in seqs:
            for hd in range(RET_HEADS):
                sfin_ref[g, d, hd] = st_ref[g, hd * RET_DK:(hd + 1) * RET_DK, hd * RET_DV:(hd + 1) * RET_DV]

    def fwd(n):
        sl = pl.ds(chunk(n), c)
        qs = [rq_ref[g, sl, :] for g in seqs]
        ks = [rk_ref[g, sl, :] for g in seqs]
        vs = [rv_ref[g, sl, :] for g in seqs]
        scores = [_dot_nt(q, stack_heads(k)) for q, k in zip(qs, ks)]
        inter, inc = state_terms(qs, ks, vs, qdec_f, kdec_f)
        for g in seqs:
            o_acc[g, sl, :] = _dot((scores[g] * dcat).astype(BF16), stack_heads(vs[g])) + inter[g]
        state_update(inc, cdec_f)

    def bwd(j):
        sl = pl.ds(chunk(n_chunks - 1 - j), c)
        qs = [rq_ref[g, sl, :] for g in seqs]
        ks = [rk_ref[g, sl, :] for g in seqs]
        vs = [rv_ref[g, sl, :] for g in seqs]
        inter, inc = state_terms(qs, ks, vs, qdec_b, kdec_b)
        os_ = [o_acc[g, sl, :] + inter[g] for g in seqs]
        sq = [o * o for o in os_]
        hi = [x.astype(BF16) for x in sq]
        ms = [_dot(h, head_mean) + _dot((x - h.astype(F32)).astype(BF16), head_mean)
              for x, h in zip(sq, hi)]
        state_update(inc, cdec_b)
        for g in seqs:
            y = os_[g] * lax.rsqrt(ms[g] + EPS)
            out_ref[g, sl, :] = (y * _silu(rg_ref[g, sl, :])).astype(BF16)

    init_state(0)
    scan(fwd)
    if not latent:
        emit_state(0)
    init_state(1)
    scan(bwd)
    if not latent:
        emit_state(1)


def _ret_call(lgl, lgc, rq, rk, rv, rg, state_ret, layer, *, group):
    b, t, _ = rq.shape
    latent = state_ret is not None
    seq_spec = pl.BlockSpec((group, t, RET_W), lambda i: (i, 0, 0))
    st_block = (2, RET_HEADS, RET_DK, RET_DV)
    in_specs = [_layer_spec((SUBLANES, RET_W), layer),
                _layer_spec((SUBLANES, RET_HEADS * RET_CHUNK), layer),
                seq_spec, seq_spec, seq_spec, seq_spec]
    args = [lgl, lgc, rq, rk, rv, rg]
    out_specs = [seq_spec]
    out_shape = [jax.ShapeDtypeStruct((b, t, RET_W), BF16)]
    if latent:
        in_specs.append(pl.BlockSpec((group, None) + st_block, lambda i: (i, layer, 0, 0, 0, 0)))
        args.append(state_ret)
    else:
        out_specs.append(pl.BlockSpec((group,) + st_block, lambda i: (i, 0, 0, 0, 0)))
        out_shape.append(jax.ShapeDtypeStruct((b,) + st_block, F32))
    return pl.pallas_call(
        functools.partial(_ret_kernel, latent=latent, n_chunks=t // RET_CHUNK, group=group),
        grid=(b // group,),
        in_specs=in_specs,
        out_specs=out_specs,
        out_shape=out_shape,
        scratch_shapes=[pltpu.VMEM((group, t, RET_W), F32), pltpu.VMEM((group, RET_W, RET_W), F32)],
        compiler_params=pltpu.CompilerParams(
            dimension_semantics=("arbitrary",), vmem_limit_bytes=VMEM_LIMIT),
        name="ret_latent" if latent else "ret_ctx",
    )(*args)


def _out_kernel(x_ref, attn_ref, conv_ref, ret_ref, mod_ref, g2_ref, wo_ref, wg_ref, wu_ref,
                wd_ref, o_ref, act_ref):
    subs = [slice(r0, r0 + OUT_SUB) for r0 in range(0, x_ref.shape[0], OUT_SUB)]
    x1 = []
    for rows in subs:
        mix = (_dot(attn_ref[rows, :], wo_ref[0:MLA_W, :])
               + _dot(conv_ref[rows, :], wo_ref[MLA_W:MLA_W + CONV_W, :])
               + _dot(ret_ref[rows, :], wo_ref[MLA_W + CONV_W:, :]))
        x1.append(x_ref[rows, :] + mod_ref[:, 2 * D_MODEL:3 * D_MODEL] * mix)
    for rows, xr in zip(subs, x1):
        h = ((_rms(xr) * g2_ref[...]) * (1.0 + mod_ref[:, 4 * D_MODEL:5 * D_MODEL])
             + mod_ref[:, 3 * D_MODEL:4 * D_MODEL]).astype(BF16)
        for j in range(FFN_HIDDEN // MXU_N):
            sl = slice(j * MXU_N, (j + 1) * MXU_N)
            gate = _dot(h, wg_ref[:, sl])
            up = _dot(h, wu_ref[:, sl])
            act_ref[rows, sl] = (_silu(gate) * up).astype(BF16)
        ffn = _dot(act_ref[rows, :], wd_ref[...])
        o_ref[rows, :] = xr + mod_ref[:, 5 * D_MODEL:6 * D_MODEL] * ffn


def _out_call(x2d, attn, conv, ret, mod, pw, layer, *, latent, seq, tm):
    rows = x2d.shape[0]
    tiles_per_seq = seq // tm
    if latent:
        mod_map = lambda i: (layer, 1 + i // tiles_per_seq, 0, 0)
    else:
        mod_map = lambda i: (layer, 0, 0, 0)
    row_spec = lambda w: pl.BlockSpec((tm, w), lambda i: (i, 0))
    return pl.pallas_call(
        _out_kernel,
        grid=(rows // tm,),
        in_specs=[
            row_spec(D_MODEL), row_spec(MLA_W), row_spec(CONV_W), row_spec(RET_W),
            pl.BlockSpec((None, None, 1, 6 * D_MODEL), mod_map),
            _layer_spec((1, D_MODEL), layer),
            _layer_spec((D_MODEL, D_MODEL), layer),
            _layer_spec((D_MODEL, FFN_HIDDEN), layer),
            _layer_spec((D_MODEL, FFN_HIDDEN), layer),
            _layer_spec((FFN_HIDDEN, D_MODEL), layer),
        ],
        out_specs=row_spec(D_MODEL),
        out_shape=jax.ShapeDtypeStruct((rows, D_MODEL), F32),
        scratch_shapes=[pltpu.VMEM((tm, FFN_HIDDEN), BF16)],
        compiler_params=pltpu.CompilerParams(
            dimension_semantics=("arbitrary",), vmem_limit_bytes=VMEM_LIMIT),
        name="out_latent" if latent else "out_ctx",
    )(x2d, attn, conv, ret, mod, pw["g2"], pw["wo"], pw["wg"], pw["wu"], pw["wd"])


def _pad_heads(w, heads, width):
    lead = w.shape[:-1]
    w = w.reshape(*lead, heads, width)
    w = jnp.pad(w, [(0, 0)] * len(lead) + [(0, 0), (0, HEAD_PAD - width)])
    return w.reshape(*lead, heads * HEAD_PAD)


def _swap_rope_pairs(w):
    nope = jnp.zeros_like(w[..., :QK_NOPE])
    rope = w[..., QK_NOPE:].reshape(*w.shape[:-1], 2, 2, ROPE_PAIR)
    rope = rope[..., ::-1, :].reshape(*w.shape[:-1], QK_ROPE)
    return jnp.concatenate([nope, rope], axis=-1)


def _prep_weights(w_in, q_norm_g, kv_norm_g, w_q_up, w_kv_up, q_head_norm_g, k_head_norm_g,
                  conv_w, norm1_g, norm2_g, w_o, w_ffn_gate, w_ffn_up, w_ffn_down):
    d = DEPTH
    zeros = lambda n: jnp.zeros((d, D_MODEL, n), w_in.dtype)
    w_in_p = jnp.concatenate([w_in[..., :C_KPE], zeros(QK_NOPE), w_in[..., C_KPE:C_KPE + QK_ROPE],
                              zeros(LANES - QK_HEAD), w_in[..., C_KPE + QK_ROPE:]], axis=-1)
    wq_h = w_q_up.reshape(d, Q_LORA, MLA_HEADS, QK_HEAD)
    wq = jnp.concatenate([_pad_heads(wq_h.reshape(d, Q_LORA, -1), MLA_HEADS, QK_HEAD),
                          _pad_heads(_swap_rope_pairs(wq_h).reshape(d, Q_LORA, -1), MLA_HEADS, QK_HEAD)],
                         axis=-1)
    kvu = w_kv_up.reshape(d, KV_LORA, MLA_HEADS, QK_NOPE + V_HEAD)
    wk = _pad_heads(kvu[..., :QK_NOPE].reshape(d, KV_LORA, -1), MLA_HEADS, QK_NOPE)
    wv = kvu[..., QK_NOPE:].reshape(d, KV_LORA, MLA_W)
    lane_pad = ((0, 0), (0, HEAD_PAD - QK_HEAD))
    qhg = jnp.stack([jnp.pad(q_head_norm_g, lane_pad),
                     jnp.pad(_swap_rope_pairs(q_head_norm_g), lane_pad)], axis=1)
    return {
        "g1": norm1_g.reshape(d, 1, D_MODEL),
        "g2": norm2_g.reshape(d, 1, D_MODEL),
        "w_in": w_in_p.astype(BF16),
        "qng": q_norm_g.reshape(d, 1, Q_LORA),
        "kvng": kv_norm_g.reshape(d, 1, KV_LORA),
        "wq": wq.astype(BF16),
        "wkv": jnp.concatenate([wk, wv], axis=-1).astype(BF16),
        "qhg": jnp.pad(qhg, ((0, 0), (0, SUBLANES - 2), (0, 0))),
        "khg": jnp.pad(k_head_norm_g, lane_pad).reshape(d, 1, LANES),
        "convw": jnp.pad(conv_w, ((0, 0), (0, SUBLANES - 3), (0, 0))),
        "wo": w_o.astype(BF16),
        "wg": w_ffn_gate.astype(BF16),
        "wu": w_ffn_up.astype(BF16),
        "wd": w_ffn_down.astype(BF16),
    }


def _rope_tables(t):
    half = QK_ROPE // 2
    freqs = jnp.power(ROPE_THETA, -jnp.arange(0, half, 2, dtype=F32) / half)
    lane = jnp.arange(LANES)
    rel = lane - QK_NOPE
    in_rope = (rel >= 0) & (rel < QK_ROPE)
    freq_lane = freqs[jnp.clip(rel, 0, QK_ROPE - 1) % ROPE_PAIR]
    tok = jnp.arange(t)
    row = (tok // GRID_W).astype(F32)
    col = (tok % GRID_W).astype(F32)
    pos = jnp.where((rel < half)[None, :], row[:, None], col[:, None])
    ang = pos * freq_lane[None, :]
    first = in_rope & ((rel % half) < ROPE_PAIR)
    second = in_rope & ((rel % half) >= ROPE_PAIR)
    cos = jnp.where(in_rope[None, :], jnp.cos(ang), 1.0)
    sin = jnp.sin(ang)
    sin_lo = jnp.where(first[None, :], -sin, 0.0)
    sin_hi = jnp.where(second[None, :], sin, 0.0)
    return cos, sin_lo, sin_hi


def _decay_lanes(ret_decay_fwd, ret_decay_bwd):
    lg = jnp.stack([jax.nn.log_sigmoid(ret_decay_fwd.astype(F32)),
                    jax.nn.log_sigmoid(ret_decay_bwd.astype(F32))], axis=1)
    pad = ((0, 0), (0, SUBLANES - 2), (0, 0))
    return (jnp.pad(jnp.repeat(lg, RET_DV, axis=2), pad),
            jnp.pad(jnp.repeat(lg, RET_CHUNK, axis=2), pad))


TM = 1024
PROJ_SUB = 256
PROJ_CONV_SLOT = 1
TM_OUT = 1024
OUT_SUB = 256
RET_GROUP = 4
TQ = 2048
TQ_UNIT = 128
CTX_ATTN_LOOKAHEAD = 2
ATTN_LOOKAHEAD = 1


def kernel(x_prompt, x_sample, cache_ckv, cache_kpe, state_ret, c, c_ctx, ada_w, ada_b, norm1_g, norm2_g, w_in, q_norm_g, kv_norm_g, w_q_up, w_kv_up, q_head_norm_g, k_head_norm_g, conv_w, ret_decay_fwd, ret_decay_bwd, w_o, w_ffn_gate, w_ffn_up, w_ffn_down):
    bp, tp, _ = x_prompt.shape
    bs, ts, _ = x_sample.shape

    cond = jnp.concatenate([c_ctx[None, :], c, jnp.zeros((SUBLANES - 1 - bs, D_MODEL), F32)], axis=0)
    mods = _ada_call(cond, ada_w, ada_b).reshape(DEPTH, SUBLANES, 1, 6 * D_MODEL)

    pw = _prep_weights(w_in, q_norm_g, kv_norm_g, w_q_up, w_kv_up, q_head_norm_g, k_head_norm_g,
                       conv_w, norm1_g, norm2_g, w_o, w_ffn_gate, w_ffn_up, w_ffn_down)
    rope_tabs = _rope_tables(ts)
    lgl, lgc = _decay_lanes(ret_decay_fwd, ret_decay_bwd)
    kpe_ctx = jnp.pad(cache_kpe, ((0, 0), (0, 0), (0, 0), (QK_NOPE, LANES - QK_HEAD)))
    kc, vc = _ctx_kv_call(cache_ckv, kpe_ctx, pw["wkv"], pw["khg"])

    xp = x_prompt.reshape(bp * tp, D_MODEL)
    xs = x_sample.reshape(bs * ts, D_MODEL)
    ckv_list, kpe_list, st_list = [], [], []
    for l in range(DEPTH):
        attn, ckv, kpe, conv, rq, rk, rv, rg = _proj_call(
            xp, mods, pw, l, None, latent=False, seq=tp, tm=TM)
        r3 = lambda a: a.reshape(bp, tp, RET_W)
        ret, s_fin = _ret_call(lgl, lgc, r3(rq), r3(rk), r3(rv), r3(rg), None, l, group=RET_GROUP)
        xp = _out_call(xp, attn, conv, ret.reshape(bp * tp, RET_W), mods, pw, l,
                       latent=False, seq=tp, tm=TM_OUT)
        ckv_list.append(ckv.reshape(bp, tp, KV_LORA))
        kpe_list.append(kpe.reshape(bp, tp, QK_ROPE))
        st_list.append(s_fin)

        q, k, v, conv, rq, rk, rv, rg = _proj_call(
            xs, mods, pw, l, rope_tabs, latent=True, seq=ts, tm=TM)
        attn = _attn_call(q.reshape(bs, ts, QK_W), k.reshape(bs, ts, QK_W),
                          v.reshape(bs, ts, MLA_W), kc, vc, l, tq=TQ, unit=TQ_UNIT)
        r3 = lambda a: a.reshape(bs, ts, RET_W)
        (ret,) = _ret_call(lgl, lgc, r3(rq), r3(rk), r3(rv), r3(rg), state_ret, l, group=RET_GROUP)
        xs = _out_call(xs, attn.reshape(bs * ts, MLA_W), conv, ret.reshape(bs * ts, RET_W),
                       mods, pw, l, latent=True, seq=ts, tm=TM_OUT)

    return (xp.reshape(bp, tp, D_MODEL), xs.reshape(bs, ts, D_MODEL),
            jnp.stack(ckv_list, axis=1), jnp.stack(kpe_list, axis=1), jnp.stack(st_list, axis=1))
```

```python
import functools

import jax
import jax.numpy as jnp
from jax import lax
from jax.experimental import pallas as pl
from jax.experimental.pallas import tpu as pltpu

D_MODEL = 1024
DEPTH = 2
GRID_W = 64
MLA_HEADS = 8
Q_LORA = 256
KV_LORA = 128
QK_NOPE = 64
QK_ROPE = 32
V_HEAD = 64
QK_HEAD = QK_NOPE + QK_ROPE
MLA_W = MLA_HEADS * V_HEAD
CONV_W = 256
RET_HEADS = 4
RET_DK = 64
RET_DV = 64
RET_W = RET_HEADS * RET_DV
RET_CHUNK = 128
FFN_HIDDEN = 2816
ROPE_THETA = 10000.0
EPS = 1e-6
LOG2_E = 1.4426950408889634

LANES = 128
SUBLANES = 8
MXU_N = 256
HEAD_PAD = LANES
QK_W = MLA_HEADS * HEAD_PAD
ROPE_PAIR = QK_ROPE // 4
VMEM_LIMIT = 56 * 1024 * 1024

C_QLAT = 0
C_KVLAT = C_QLAT + Q_LORA
C_KPE = C_KVLAT + KV_LORA
C_GB = C_KPE + LANES
C_GC = C_GB + CONV_W
C_XIN = C_GC + CONV_W
C_RQ = C_XIN + CONV_W
C_RK = C_RQ + RET_W
C_RV = C_RK + RET_W
C_RG = C_RV + RET_W
IN_COLS_PAD = C_RG + RET_W

BF16 = jnp.bfloat16
F32 = jnp.float32


def _dot(a, b):
    return jnp.dot(a, b, preferred_element_type=F32)


def _dot_nt(a, b):
    return lax.dot_general(a, b, (((1,), (1,)), ((), ())), preferred_element_type=F32)


def _rms(x, n=None):
    n = x.shape[-1] if n is None else n
    return x * lax.rsqrt(jnp.sum(x * x, axis=-1, keepdims=True) * (1.0 / n) + EPS)


def _silu(x):
    return x * (1.0 / (1.0 + jnp.exp(-x)))


def _const_spec(shape):
    nd = len(shape)
    return pl.BlockSpec(shape, lambda *_: (0,) * nd, pipeline_mode=pl.Buffered(1))


def _layer_spec(shape, layer):
    nd = len(shape)
    return pl.BlockSpec((None,) + tuple(shape), lambda *_: (layer,) + (0,) * nd,
                        pipeline_mode=pl.Buffered(1))


def _softmax_pv(s_list, v_list):
    m = functools.reduce(jnp.maximum, [jnp.max(s, axis=-1, keepdims=True) for s in s_list])
    l, o = None, None
    for s, v in zip(s_list, v_list):
        p = jnp.exp2(s - m)
        ls = jnp.sum(p, axis=-1, keepdims=True)
        os_ = _dot(p.astype(BF16), v)
        l = ls if l is None else l + ls
        o = os_ if o is None else o + os_
    return o * (1.0 / l)


def _merge_head_pair(o0, o1):
    lane = lax.broadcasted_iota(jnp.int32, o0.shape, 1)
    return jnp.where(lane < V_HEAD, o0, o1).astype(BF16)


ADA_TN = 1536


def _ada_kernel(cond_ref, w_ref, b_ref, o_ref):
    a = _silu(cond_ref[...]).astype(BF16)
    o_ref[...] = _dot(a, w_ref[...].astype(BF16)) + b_ref[...]


def _ada_call(cond, ada_w, ada_b):
    n = 6 * D_MODEL
    return pl.pallas_call(
        _ada_kernel,
        grid=(DEPTH, n // ADA_TN),
        in_specs=[
            pl.BlockSpec((SUBLANES, D_MODEL), lambda l, j: (0, 0)),
            pl.BlockSpec((None, D_MODEL, ADA_TN), lambda l, j: (l, 0, j)),
            pl.BlockSpec((None, 1, ADA_TN), lambda l, j: (l, 0, j)),
        ],
        out_specs=pl.BlockSpec((None, SUBLANES, ADA_TN), lambda l, j: (l, 0, j)),
        out_shape=jax.ShapeDtypeStruct((DEPTH, SUBLANES, n), F32),
        compiler_params=pltpu.CompilerParams(
            dimension_semantics=("arbitrary", "arbitrary"), vmem_limit_bytes=VMEM_LIMIT),
        name="ada_mod",
    )(cond, ada_w, ada_b.reshape(DEPTH, 1, n))


def _key_slab(kn, shared, pe_ssq, g):
    ssq = jnp.sum(kn * kn, axis=-1, keepdims=True) + pe_ssq
    return ((kn * g + shared) * lax.rsqrt(ssq * (1.0 / QK_HEAD) + EPS)).astype(BF16)


def _ctx_kv_kernel(ckv_ref, kpe_ref, wkv_ref, khg_ref, k_ref, v_ref):
    kv = _dot(ckv_ref[...].astype(BF16), wkv_ref[...])
    kpe = kpe_ref[...]
    g = khg_ref[...]
    shared = kpe * g
    pe_ssq = jnp.sum(kpe * kpe, axis=-1, keepdims=True)
    for hd in range(MLA_HEADS):
        sl = slice(hd * HEAD_PAD, (hd + 1) * HEAD_PAD)
        k_ref[:, sl] = _key_slab(kv[:, sl], shared, pe_ssq, g)
    v_ref[...] = kv[:, QK_W:].astype(BF16)


def _ctx_kv_call(cache_ckv, kpe_slab, wkv, khg):
    b, _, t, _ = cache_ckv.shape
    return pl.pallas_call(
        _ctx_kv_kernel,
        grid=(DEPTH, b),
        in_specs=[
            pl.BlockSpec((None, None, t, KV_LORA), lambda l, i: (i, l, 0, 0)),
            pl.BlockSpec((None, None, t, LANES), lambda l, i: (i, l, 0, 0)),
            pl.BlockSpec((None, KV_LORA, QK_W + MLA_W), lambda l, i: (l, 0, 0)),
            pl.BlockSpec((None, 1, LANES), lambda l, i: (l, 0, 0)),
        ],
        out_specs=[
            pl.BlockSpec((None, None, t, QK_W), lambda l, i: (l, i, 0, 0)),
            pl.BlockSpec((None, None, t, MLA_W), lambda l, i: (l, i, 0, 0)),
        ],
        out_shape=[
            jax.ShapeDtypeStruct((DEPTH, b, t, QK_W), BF16),
            jax.ShapeDtypeStruct((DEPTH, b, t, MLA_W), BF16),
        ],
        compiler_params=pltpu.CompilerParams(
            dimension_semantics=("arbitrary", "arbitrary"), vmem_limit_bytes=VMEM_LIMIT),
        name="ctx_kv",
    )(cache_ckv, kpe_slab, wkv, khg)


def _proj_kernel(*refs, latent, tiles_per_seq, tm):
    it = iter(refs)
    x_ref = next(it)
    if latent:
        xp_ref, xn_ref = next(it), next(it)
    mod_ref, g1_ref, win_ref, qng_ref, kvng_ref = (next(it) for _ in range(5))
    wq_ref, wkv_ref, qhg_ref, khg_ref, cw_ref = (next(it) for _ in range(5))
    if latent:
        cos_ref, slo_ref, shi_ref = next(it), next(it), next(it)
        q_out, k_out, v_out = next(it), next(it), next(it)
    else:
        next(it), next(it)
        attn_out, ckv_out, kpe_out = next(it), next(it), next(it)
    conv_out, rq_out, rk_out, rv_out, rg_out = (next(it) for _ in range(5))

    shift = mod_ref[:, 0:D_MODEL]
    scale1 = 1.0 + mod_ref[:, D_MODEL:2 * D_MODEL]
    g1 = g1_ref[...]

    def norm_mod(x):
        return ((_rms(x) * g1) * scale1 + shift).astype(BF16)

    q_scale = QK_HEAD ** -0.5 * LOG2_E
    qg = qhg_ref[0:1, :] * q_scale
    kg = khg_ref[...]
    head_slice = lambda hd: slice(hd * HEAD_PAD, (hd + 1) * HEAD_PAD)
    n_sub = tm // PROJ_SUB
    subs = [slice(t * PROJ_SUB, (t + 1) * PROJ_SUB) for t in range(n_sub)]
    state = [dict() for _ in subs]

    def projection_stage(t):
        rows, st = subs[t], state[t]

        def proj(col, width):
            return _dot(st["h"], win_ref[:, col:col + width])

        def latents():
            st["h"] = norm_mod(x_ref[rows, :])
            st["q_lat"] = proj(C_QLAT, Q_LORA)
            st["kv_pe"] = proj(C_KVLAT, KV_LORA + LANES)

        def conv_inputs():
            st["u"] = proj(C_GC, CONV_W) * proj(C_XIN, CONV_W)
            if latent and t == 0:
                hh = norm_mod(jnp.concatenate([xp_ref[...], xn_ref[...]], axis=0))
                halo = (_dot(hh, win_ref[:, C_GC:C_GC + CONV_W])
                        * _dot(hh, win_ref[:, C_XIN:C_XIN + CONV_W]))
                i = pl.program_id(0)
                has_prev = jnp.where(i % tiles_per_seq != 0, 1.0, 0.0)
                has_next = jnp.where(i % tiles_per_seq != tiles_per_seq - 1, 1.0, 0.0)
                state[0]["u_before"] = halo[SUBLANES - 1:SUBLANES, :] * has_prev
                state[-1]["u_after"] = halo[SUBLANES:SUBLANES + 1, :] * has_next

        def queries():
            qn = (_rms(st["q_lat"]) * qng_ref[...]).astype(BF16)
            st["q"] = _dot(qn, wq_ref[:, 0:QK_W])
            if latent:
                st["q_sw"] = _dot(qn, wq_ref[:, QK_W:2 * QK_W])
                cos = cos_ref[rows, :]
                st["q_tab"] = qg * cos
                st["q_tab_sw"] = (qhg_ref[1:2, :] * q_scale) * (slo_ref[rows, :] + shi_ref[rows, :])

        def keys_values():
            ckv = _rms(st["kv_pe"][:, 0:KV_LORA]) * kvng_ref[...]
            kpe = st["kv_pe"][:, KV_LORA:]
            st["kv"] = _dot(ckv.astype(BF16), wkv_ref[...])
            shared = kpe * kg
            if latent:
                shared = (shared * cos_ref[rows, :]
                          + pltpu.roll(shared, LANES - ROPE_PAIR, axis=1) * slo_ref[rows, :]
                          + pltpu.roll(shared, ROPE_PAIR, axis=1) * shi_ref[rows, :])
            else:
                ckv_out[t] = ckv
                kpe_out[t] = kpe[:, QK_NOPE:QK_HEAD]
            st["shared"] = shared
            st["pe_ssq"] = jnp.sum(kpe * kpe, axis=-1, keepdims=True)

        def conv_gate():
            st["gb"] = proj(C_GB, CONV_W)

        def ret_q():
            rq_out[rows, :] = proj(C_RQ, RET_W).astype(BF16)

        def ret_k():
            rk_out[rows, :] = (proj(C_RK, RET_W) * (RET_DK ** -0.5)).astype(BF16)

        def ret_v():
            rv_out[rows, :] = proj(C_RV, RET_W).astype(BF16)

        def ret_g():
            rg_out[rows, :] = proj(C_RG, RET_W)

        return [latents, conv_inputs, queries, keys_values, conv_gate, ret_q, ret_k, ret_v, ret_g]

    def conv_finish(t):
        st, u = state[t], state[t]["u"]
        zero = jnp.zeros((1, CONV_W), F32)
        if latent:
            u_before = st["u_before"] if t == 0 else state[t - 1]["u"][PROJ_SUB - 1:PROJ_SUB, :]
            u_after = st["u_after"] if t == n_sub - 1 else state[t + 1]["u"][0:1, :]
        else:
            u_before, u_after = zero, zero
        row = lax.broadcasted_iota(jnp.int32, (PROJ_SUB, CONV_W), 0)
        u_m1 = jnp.where(row == 0, u_before, pltpu.roll(u, 1, axis=0))
        u_p1 = jnp.where(row == PROJ_SUB - 1, u_after, pltpu.roll(u, PROJ_SUB - 1, axis=0))
        y = u_m1 * cw_ref[0:1, :] + u * cw_ref[1:2, :] + u_p1 * cw_ref[2:3, :]
        conv_out[subs[t], :] = (st["gb"] * y).astype(BF16)

    def head_stage(t):
        rows, st = subs[t], state[t]

        def query_slab(sl):
            qs = st["q"][:, sl]
            r = lax.rsqrt(jnp.sum(qs * qs, axis=-1, keepdims=True) * (1.0 / QK_HEAD) + EPS)
            if latent:
                return ((qs * st["q_tab"] + st["q_sw"][:, sl] * st["q_tab_sw"]) * r).astype(BF16)
            return (qs * qg * r).astype(BF16)

        def key_slab(sl):
            return _key_slab(st["kv"][:, sl], st["shared"], st["pe_ssq"], kg)

        if latent:
            def store_head(hd):
                sl = head_slice(hd)
                q_out[rows, sl] = query_slab(sl)
                k_out[rows, sl] = key_slab(sl)
                if hd == MLA_HEADS - 1:
                    v_out[rows, :] = st["kv"][:, QK_W:].astype(BF16)
            return [functools.partial(store_head, hd) for hd in range(MLA_HEADS)]

        ahead, pair = [], []

        def scores(hd):
            sl = head_slice(hd)
            return _dot_nt(query_slab(sl), key_slab(sl))

        def attend(hd):
            if hd == 0:
                ahead.extend(scores(j) for j in range(CTX_ATTN_LOOKAHEAD))
            s = ahead.pop(0)
            if hd + CTX_ATTN_LOOKAHEAD < MLA_HEADS:
                ahead.append(scores(hd + CTX_ATTN_LOOKAHEAD))
            hp = hd // 2
            v_pair = st["kv"][:, QK_W + hp * LANES:QK_W + (hp + 1) * LANES].astype(BF16)
            pair.append(_softmax_pv([s], [v_pair]))
            if hd % 2 == 1:
                attn_out[rows, hp * LANES:(hp + 1) * LANES] = _merge_head_pair(*pair)
                pair.clear()

        return [functools.partial(attend, hd) for hd in range(MLA_HEADS)]

    for piece in projection_stage(0):
        piece()
    for t in range(n_sub):
        heads = head_stage(t)
        nxt = projection_stage(t + 1) if t + 1 < n_sub else []
        for i in range(max(len(heads), len(nxt))):
            if i < len(nxt):
                nxt[i]()
            if i < len(heads):
                heads[i]()
            if i == PROJ_CONV_SLOT and t > 0:
                conv_finish(t - 1)
    conv_finish(n_sub - 1)


def _proj_call(x2d, mod, pw, layer, rope_tabs, cache_bufs, *, latent, seq, tm):
    rows = x2d.shape[0]
    tiles_per_seq = seq // tm
    n_tiles = rows // tm
    if latent:
        mod_map = lambda i: (layer, 1 + i // tiles_per_seq, 0, 0)
    else:
        assert seq == PROJ_SUB, "the context path runs one sequence's attention per sub-tile"
        mod_map = lambda i: (layer, 0, 0, 0)
    row_spec = lambda w: pl.BlockSpec((tm, w), lambda i: (i, 0))
    in_specs = [row_spec(D_MODEL)]
    args = [x2d]
    if latent:
        blk = tm // SUBLANES
        last = rows // SUBLANES - 1
        in_specs += [
            pl.BlockSpec((SUBLANES, D_MODEL), lambda i: (jnp.maximum(i * blk - 1, 0), 0)),
            pl.BlockSpec((SUBLANES, D_MODEL), lambda i: (jnp.minimum((i + 1) * blk, last), 0)),
        ]
        args += [x2d, x2d]
    in_specs += [
        pl.BlockSpec((None, None, 1, 6 * D_MODEL), mod_map),
        _layer_spec((1, D_MODEL), layer),
        _layer_spec((D_MODEL, IN_COLS_PAD), layer),
        _layer_spec((1, Q_LORA), layer),
        _layer_spec((1, KV_LORA), layer),
        _layer_spec((Q_LORA, 2 * QK_W), layer),
        _layer_spec((KV_LORA, QK_W + MLA_W), layer),
        _layer_spec((SUBLANES, LANES), layer),
        _layer_spec((1, LANES), layer),
        _layer_spec((SUBLANES, CONV_W), layer),
    ]
    args += [mod, pw["g1"], pw["w_in"], pw["qng"], pw["kvng"], pw["wq"], pw["wkv"],
             pw["qhg"], pw["khg"], pw["convw"]]
    if latent:
        in_specs += [pl.BlockSpec((tm, LANES), lambda i: (i % tiles_per_seq, 0))] * 3
        args += list(rope_tabs)
        outs = [(QK_W, BF16), (QK_W, BF16), (MLA_W, BF16)]
        aliases = {}
    else:
        outs = [(MLA_W, BF16)]
    out_specs = [row_spec(w) for w, _ in outs]
    out_shape = [jax.ShapeDtypeStruct((rows, w), dt) for w, dt in outs]
    if not latent:
        aliases = {len(args) + j: len(outs) + j for j in range(len(cache_bufs))}
        for buf in cache_bufs:
            in_specs.append(pl.BlockSpec(memory_space=pl.ANY))
            args.append(buf)
            out_specs.append(pl.BlockSpec((tm // seq, None, seq, buf.shape[-1]),
                                          lambda i: (i, layer, 0, 0)))
            out_shape.append(jax.ShapeDtypeStruct(buf.shape, buf.dtype))
    tail = [(CONV_W, BF16), (RET_W, BF16), (RET_W, BF16), (RET_W, BF16), (RET_W, F32)]
    out_specs += [row_spec(w) for w, _ in tail]
    out_shape += [jax.ShapeDtypeStruct((rows, w), dt) for w, dt in tail]
    return pl.pallas_call(
        functools.partial(_proj_kernel, latent=latent, tiles_per_seq=tiles_per_seq, tm=tm),
        grid=(n_tiles,),
        in_specs=in_specs,
        out_specs=out_specs,
        out_shape=out_shape,
        input_output_aliases=aliases,
        compiler_params=pltpu.CompilerParams(
            dimension_semantics=("arbitrary",), vmem_limit_bytes=VMEM_LIMIT),
        name="proj_latent" if latent else "proj_ctx",
    )(*args)


def _attn_kernel(q_ref, k_ref, v_ref, kc_ref, vc_ref, o_ref, *, unit):
    units = [(r0, j) for r0 in range(0, q_ref.shape[0], unit) for j in range(2)]

    def scores(r0, j):
        sl = slice(j * HEAD_PAD, (j + 1) * HEAD_PAD)
        q = q_ref[r0:r0 + unit, sl]
        return [_dot_nt(q, k_ref[:, sl]), _dot_nt(q, kc_ref[:, sl])]

    ahead = [scores(*u) for u in units[:ATTN_LOOKAHEAD]]
    pair = []
    for idx, (r0, j) in enumerate(units):
        s = ahead.pop(0)
        if idx + ATTN_LOOKAHEAD < len(units):
            ahead.append(scores(*units[idx + ATTN_LOOKAHEAD]))
        pair.append(_softmax_pv(s, [v_ref[...], vc_ref[...]]))
        if j == 1:
            o_ref[r0:r0 + unit, :] = _merge_head_pair(*pair)
            pair = []


def _attn_call(q, k, v, kc, vc, layer, *, tq, unit):
    b, t, _ = q.shape
    tk, tc = k.shape[1], kc.shape[2]
    return pl.pallas_call(
        functools.partial(_attn_kernel, unit=unit),
        grid=(b, MLA_HEADS // 2, t // tq),
        in_specs=[
            pl.BlockSpec((None, tq, 2 * HEAD_PAD), lambda bi, hp, i: (bi, i, hp)),
            pl.BlockSpec((None, tk, 2 * HEAD_PAD), lambda bi, hp, i: (bi, 0, hp)),
            pl.BlockSpec((None, tk, 2 * V_HEAD), lambda bi, hp, i: (bi, 0, hp)),
            pl.BlockSpec((None, None, tc, 2 * HEAD_PAD), lambda bi, hp, i: (layer, bi, 0, hp)),
            pl.BlockSpec((None, None, tc, 2 * V_HEAD), lambda bi, hp, i: (layer, bi, 0, hp)),
        ],
        out_specs=pl.BlockSpec((None, tq, 2 * V_HEAD), lambda bi, hp, i: (bi, i, hp)),
        out_shape=jax.ShapeDtypeStruct((b, t, MLA_W), BF16),
        compiler_params=pltpu.CompilerParams(
            dimension_semantics=("arbitrary", "arbitrary", "arbitrary"),
            vmem_limit_bytes=VMEM_LIMIT),
        name="attn_latent",
    )(q, k, v, kc, vc)


def _ret_kernel(*refs, latent, n_chunks, group):
    if latent:
        lgl_ref, lgc_ref, rq_ref, rk_ref, rv_ref, rg_ref, s0_ref, out_ref, o_acc, st_ref = refs
    else:
        lgl_ref, lgc_ref, rq_ref, rk_ref, rv_ref, rg_ref, _, out_ref, sfin_ref, o_acc, st_ref = refs
    c = RET_CHUNK
    seqs = range(group)
    lg_f, lg_b = lgl_ref[0:1, :], lgl_ref[1:2, :]
    pos = lax.broadcasted_iota(jnp.int32, (c, RET_W), 0).astype(F32)
    qdec_f = jnp.exp(lg_f * (pos + 1.0)).astype(BF16)
    kdec_f = jnp.exp(lg_f * (c - 1.0 - pos)).astype(BF16)
    qdec_b = jnp.exp(lg_b * (c - pos)).astype(BF16)
    kdec_b = jnp.exp(lg_b * pos).astype(BF16)
    cdec_f = jnp.exp(lg_f * float(c))
    cdec_b = jnp.exp(lg_b * float(c))
    ri = lax.broadcasted_iota(jnp.int32, (c, RET_HEADS * c), 0)
    ci = jnp.bitwise_and(lax.broadcasted_iota(jnp.int32, (c, RET_HEADS * c), 1), c - 1)
    diff = (ri - ci).astype(F32)
    dcat = (jnp.where(diff >= 0, jnp.exp(lgc_ref[0:1, :] * diff), 0.0)
            + jnp.where(diff <= 0, jnp.exp(lgc_ref[1:2, :] * (-diff)), 0.0))
    head_shift = RET_DV.bit_length() - 1
    lane_head = jnp.right_shift(lax.broadcasted_iota(jnp.int32, (c, RET_W), 1), head_shift)
    r_head = jnp.right_shift(lax.broadcasted_iota(jnp.int32, (RET_W, RET_W), 0), head_shift)
    c_head = jnp.right_shift(lax.broadcasted_iota(jnp.int32, (RET_W, RET_W), 1), head_shift)
    same_head = r_head == c_head
    head_mean = jnp.where(same_head, 1.0 / RET_DV, 0.0).astype(BF16)

    def stack_heads(a):
        zero = jnp.zeros_like(a)
        return jnp.concatenate([jnp.where(lane_head == hd, a, zero) for hd in range(RET_HEADS)], axis=0)

    def chunk(n):
        return n * c if isinstance(n, int) else pl.multiple_of(n * c, c)

    def load(sl):
        return ([rq_ref[g, sl, :] for g in seqs], [rk_ref[g, sl, :] for g in seqs],
                [rv_ref[g, sl, :] for g in seqs])

    def state_terms(d, qs, ks, vs, qdec, kdec):
        inc = [_dot((k * kdec).T, v) for k, v in zip(ks, vs)]
        zero = jnp.zeros((RET_W, RET_W), BF16)
        inter = [_dot(q * qdec, jnp.where(same_head, st_ref[g, d].astype(BF16), zero))
                 for g, q in zip(seqs, qs)]
        return inter, inc

    def state_update(d, inc, cdec):
        for g in seqs:
            st_ref[g, d] = st_ref[g, d] * cdec + inc[g]

    def finish(sl, os_):
        sq = [o * o for o in os_]
        hi = [x.astype(BF16) for x in sq]
        ms = [_dot(h, head_mean) + _dot((x - h.astype(F32)).astype(BF16), head_mean)
              for x, h in zip(sq, hi)]
        for g in seqs:
            y = os_[g] * lax.rsqrt(ms[g] + EPS)
            out_ref[g, sl, :] = (y * _silu(rg_ref[g, sl, :])).astype(BF16)

    def step(j, second_half):
        sl_f = pl.ds(chunk(j), c)
        sl_b = pl.ds(chunk(n_chunks - 1 - j), c)
        qf, kf, vf = load(sl_f)
        qb, kb, vb = load(sl_b)
        scores = [_dot_nt(q, stack_heads(k)) for q, k in zip(qf, kf)]
        inter_f, inc_f = state_terms(0, qf, kf, vf, qdec_f, kdec_f)
        inter_b, inc_b = state_terms(1, qb, kb, vb, qdec_b, kdec_b)
        o_f = [_dot((scores[g] * dcat).astype(BF16), stack_heads(vf[g])) + inter_f[g] for g in seqs]
        state_update(0, inc_f, cdec_f)
        state_update(1, inc_b, cdec_b)
        if second_half:
            finish(sl_f, [o_acc[g, sl_f, :] + o_f[g] for g in seqs])
            finish(sl_b, [o_acc[g, sl_b, :] + inter_b[g] for g in seqs])
        else:
            for g in seqs:
                o_acc[g, sl_f, :] = o_f[g]
                o_acc[g, sl_b, :] = inter_b[g]

    def scan(lo, hi, second_half):
        if hi - lo <= 1:
            for j in range(lo, hi):
                step(j, second_half)
        else:
            lax.fori_loop(lo, hi, lambda j, carry: (step(j, second_half), carry)[1], 0)

    st_ref[...] = jnp.zeros_like(st_ref)
    if latent:
        for g in seqs:
            for d in range(2):
                for hd in range(RET_HEADS):
                    st_ref[g, d, hd * RET_DK:(hd + 1) * RET_DK, hd * RET_DV:(hd + 1) * RET_DV] = s0_ref[g, d, hd]
    half = n_chunks // 2
    scan(0, half, False)
    scan(half, n_chunks, True)
    if not latent:
        for g in seqs:
            for d in range(2):
                for hd in range(RET_HEADS):
                    sfin_ref[g, d, hd] = st_ref[g, d, hd * RET_DK:(hd + 1) * RET_DK, hd * RET_DV:(hd + 1) * RET_DV]


def _ret_call(lgl, lgc, rq, rk, rv, rg, state_ret, layer, *, latent, group):
    b, t, _ = rq.shape
    assert (t // RET_CHUNK) % 2 == 0, "the two scans meet in the middle of an even chunk count"
    seq_spec = pl.BlockSpec((group, t, RET_W), lambda i: (i, 0, 0))
    st_block = (2, RET_HEADS, RET_DK, RET_DV)
    in_specs = [_layer_spec((SUBLANES, RET_W), layer),
                _layer_spec((SUBLANES, RET_HEADS * RET_CHUNK), layer),
                seq_spec, seq_spec, seq_spec, seq_spec]
    args = [lgl, lgc, rq, rk, rv, rg]
    out_specs = [seq_spec]
    out_shape = [jax.ShapeDtypeStruct((b, t, RET_W), BF16)]
    st_spec = pl.BlockSpec((group, None) + st_block, lambda i: (i, layer, 0, 0, 0, 0))
    aliases = {}
    if latent:
        in_specs.append(st_spec)
    else:
        in_specs.append(pl.BlockSpec(memory_space=pl.ANY))
        aliases = {len(args): 1}
        out_specs.append(st_spec)
        out_shape.append(jax.ShapeDtypeStruct(state_ret.shape, state_ret.dtype))
    args.append(state_ret)
    return pl.pallas_call(
        functools.partial(_ret_kernel, latent=latent, n_chunks=t // RET_CHUNK, group=group),
        grid=(b // group,),
        in_specs=in_specs,
        out_specs=out_specs,
        out_shape=out_shape,
        input_output_aliases=aliases,
        scratch_shapes=[pltpu.VMEM((group, t, RET_W), F32),
                        pltpu.VMEM((group, 2, RET_W, RET_W), F32)],
        compiler_params=pltpu.CompilerParams(
            dimension_semantics=("arbitrary",), vmem_limit_bytes=VMEM_LIMIT),
        name="ret_latent" if latent else "ret_ctx",
    )(*args)


def _out_kernel(x_ref, attn_ref, conv_ref, ret_ref, mod_ref, g2_ref, wo_ref, wg_ref, wu_ref,
                wd_ref, o_ref, act_ref):
    subs = [slice(r0, r0 + OUT_SUB) for r0 in range(0, x_ref.shape[0], OUT_SUB)]
    x1 = []
    for rows in subs:
        mix = (_dot(attn_ref[rows, :], wo_ref[0:MLA_W, :])
               + _dot(conv_ref[rows, :], wo_ref[MLA_W:MLA_W + CONV_W, :])
               + _dot(ret_ref[rows, :], wo_ref[MLA_W + CONV_W:, :]))
        x1.append(x_ref[rows, :] + mod_ref[:, 2 * D_MODEL:3 * D_MODEL] * mix)
    for rows, xr in zip(subs, x1):
        h = ((_rms(xr) * g2_ref[...]) * (1.0 + mod_ref[:, 4 * D_MODEL:5 * D_MODEL])
             + mod_ref[:, 3 * D_MODEL:4 * D_MODEL]).astype(BF16)
        for j in range(FFN_HIDDEN // MXU_N):
            sl = slice(j * MXU_N, (j + 1) * MXU_N)
            gate = _dot(h, wg_ref[:, sl])
            up = _dot(h, wu_ref[:, sl])
            act_ref[rows, sl] = (_silu(gate) * up).astype(BF16)
        ffn = _dot(act_ref[rows, :], wd_ref[...])
        o_ref[rows, :] = xr + mod_ref[:, 5 * D_MODEL:6 * D_MODEL] * ffn


def _out_call(x2d, attn, conv, ret, mod, pw, layer, *, latent, seq, tm):
    rows = x2d.shape[0]
    tiles_per_seq = seq // tm
    if latent:
        mod_map = lambda i: (layer, 1 + i // tiles_per_seq, 0, 0)
    else:
        mod_map = lambda i: (layer, 0, 0, 0)
    row_spec = lambda w: pl.BlockSpec((tm, w), lambda i: (i, 0))
    return pl.pallas_call(
        _out_kernel,
        grid=(rows // tm,),
        in_specs=[
            row_spec(D_MODEL), row_spec(MLA_W), row_spec(CONV_W), row_spec(RET_W),
            pl.BlockSpec((None, None, 1, 6 * D_MODEL), mod_map),
            _layer_spec((1, D_MODEL), layer),
            _layer_spec((D_MODEL, D_MODEL), layer),
            _layer_spec((D_MODEL, FFN_HIDDEN), layer),
            _layer_spec((D_MODEL, FFN_HIDDEN), layer),
            _layer_spec((FFN_HIDDEN, D_MODEL), layer),
        ],
        out_specs=row_spec(D_MODEL),
        out_shape=jax.ShapeDtypeStruct((rows, D_MODEL), F32),
        scratch_shapes=[pltpu.VMEM((tm, FFN_HIDDEN), BF16)],
        compiler_params=pltpu.CompilerParams(
            dimension_semantics=("arbitrary",), vmem_limit_bytes=VMEM_LIMIT),
        name="out_latent" if latent else "out_ctx",
    )(x2d, attn, conv, ret, mod, pw["g2"], pw["wo"], pw["wg"], pw["wu"], pw["wd"])


def _pad_heads(w, heads, width):
    lead = w.shape[:-1]
    w = w.reshape(*lead, heads, width)
    w = jnp.pad(w, [(0, 0)] * len(lead) + [(0, 0), (0, HEAD_PAD - width)])
    return w.reshape(*lead, heads * HEAD_PAD)


def _swap_rope_pairs(w):
    nope = jnp.zeros_like(w[..., :QK_NOPE])
    rope = w[..., QK_NOPE:].reshape(*w.shape[:-1], 2, 2, ROPE_PAIR)
    rope = rope[..., ::-1, :].reshape(*w.shape[:-1], QK_ROPE)
    return jnp.concatenate([nope, rope], axis=-1)


W_IN_ROWS = 256


def _w_in_layout_kernel(w_ref, o_ref):
    n = w_ref.shape[0]
    o_ref[:, 0:C_KPE] = w_ref[:, 0:C_KPE].astype(BF16)
    o_ref[:, C_KPE:C_GB] = jnp.concatenate(
        [jnp.zeros((n, QK_NOPE), F32), w_ref[:, C_KPE:C_KPE + QK_ROPE],
         jnp.zeros((n, LANES - QK_HEAD), F32)], axis=1).astype(BF16)
    o_ref[:, C_GB:] = w_ref[:, C_KPE + QK_ROPE:].astype(BF16)


def _w_in_layout_call(w_in):
    d, k, n = w_in.shape
    return pl.pallas_call(
        _w_in_layout_kernel,
        grid=(d, k // W_IN_ROWS),
        in_specs=[pl.BlockSpec((None, W_IN_ROWS, n), lambda l, i: (l, i, 0))],
        out_specs=pl.BlockSpec((None, W_IN_ROWS, IN_COLS_PAD), lambda l, i: (l, i, 0)),
        out_shape=jax.ShapeDtypeStruct((d, k, IN_COLS_PAD), BF16),
        compiler_params=pltpu.CompilerParams(
            dimension_semantics=("arbitrary", "arbitrary"), vmem_limit_bytes=VMEM_LIMIT),
        name="w_in_layout",
    )(w_in)


def _prep_weights(w_in, q_norm_g, kv_norm_g, w_q_up, w_kv_up, q_head_norm_g, k_head_norm_g,
                  conv_w, norm1_g, norm2_g, w_o, w_ffn_gate, w_ffn_up, w_ffn_down):
    d = DEPTH
    wq_h = w_q_up.reshape(d, Q_LORA, MLA_HEADS, QK_HEAD)
    wq = jnp.concatenate([_pad_heads(wq_h.reshape(d, Q_LORA, -1), MLA_HEADS, QK_HEAD),
                          _pad_heads(_swap_rope_pairs(wq_h).reshape(d, Q_LORA, -1), MLA_HEADS, QK_HEAD)],
                         axis=-1)
    kvu = w_kv_up.reshape(d, KV_LORA, MLA_HEADS, QK_NOPE + V_HEAD)
    wk = _pad_heads(kvu[..., :QK_NOPE].reshape(d, KV_LORA, -1), MLA_HEADS, QK_NOPE)
    wv = kvu[..., QK_NOPE:].reshape(d, KV_LORA, MLA_W)
    lane_pad = ((0, 0), (0, HEAD_PAD - QK_HEAD))
    qhg = jnp.stack([jnp.pad(q_head_norm_g, lane_pad),
                     jnp.pad(_swap_rope_pairs(q_head_norm_g), lane_pad)], axis=1)
    return {
        "g1": norm1_g.reshape(d, 1, D_MODEL),
        "g2": norm2_g.reshape(d, 1, D_MODEL),
        "w_in": _w_in_layout_call(w_in),
        "qng": q_norm_g.reshape(d, 1, Q_LORA),
        "kvng": kv_norm_g.reshape(d, 1, KV_LORA),
        "wq": wq.astype(BF16),
        "wkv": jnp.concatenate([wk, wv], axis=-1).astype(BF16),
        "qhg": jnp.pad(qhg, ((0, 0), (0, SUBLANES - 2), (0, 0))),
        "khg": jnp.pad(k_head_norm_g, lane_pad).reshape(d, 1, LANES),
        "convw": jnp.pad(conv_w, ((0, 0), (0, SUBLANES - 3), (0, 0))),
        "wo": w_o.astype(BF16),
        "wg": w_ffn_gate.astype(BF16),
        "wu": w_ffn_up.astype(BF16),
        "wd": w_ffn_down.astype(BF16),
    }


def _rope_tables(t):
    half = QK_ROPE // 2
    freqs = jnp.power(ROPE_THETA, -jnp.arange(0, half, 2, dtype=F32) / half)
    lane = jnp.arange(LANES)
    rel = lane - QK_NOPE
    in_rope = (rel >= 0) & (rel < QK_ROPE)
    freq_lane = freqs[jnp.clip(rel, 0, QK_ROPE - 1) % ROPE_PAIR]
    tok = jnp.arange(t)
    row = (tok // GRID_W).astype(F32)
    col = (tok % GRID_W).astype(F32)
    pos = jnp.where((rel < half)[None, :], row[:, None], col[:, None])
    ang = pos * freq_lane[None, :]
    first = in_rope & ((rel % half) < ROPE_PAIR)
    second = in_rope & ((rel % half) >= ROPE_PAIR)
    cos = jnp.where(in_rope[None, :], jnp.cos(ang), 1.0)
    sin = jnp.sin(ang)
    sin_lo = jnp.where(first[None, :], -sin, 0.0)
    sin_hi = jnp.where(second[None, :], sin, 0.0)
    return cos, sin_lo, sin_hi


def _decay_lanes(ret_decay_fwd, ret_decay_bwd):
    lg = jnp.stack([jax.nn.log_sigmoid(ret_decay_fwd.astype(F32)),
                    jax.nn.log_sigmoid(ret_decay_bwd.astype(F32))], axis=1)
    pad = ((0, 0), (0, SUBLANES - 2), (0, 0))
    return (jnp.pad(jnp.repeat(lg, RET_DV, axis=2), pad),
            jnp.pad(jnp.repeat(lg, RET_CHUNK, axis=2), pad))


TM = 1024
PROJ_SUB = 256
PROJ_CONV_SLOT = 1
TM_OUT = 1024
OUT_SUB = 256
RET_GROUP = 4
RET_GROUP_LATENT = 2
TQ = 2048
TQ_UNIT = 128
CTX_ATTN_LOOKAHEAD = 2
ATTN_LOOKAHEAD = 1


def kernel(x_prompt, x_sample, cache_ckv, cache_kpe, state_ret, c, c_ctx, ada_w, ada_b, norm1_g, norm2_g, w_in, q_norm_g, kv_norm_g, w_q_up, w_kv_up, q_head_norm_g, k_head_norm_g, conv_w, ret_decay_fwd, ret_decay_bwd, w_o, w_ffn_gate, w_ffn_up, w_ffn_down):
    bp, tp, _ = x_prompt.shape
    bs, ts, _ = x_sample.shape

    cond = jnp.concatenate([c_ctx[None, :], c, jnp.zeros((SUBLANES - 1 - bs, D_MODEL), F32)], axis=0)
    mods = _ada_call(cond, ada_w, ada_b).reshape(DEPTH, SUBLANES, 1, 6 * D_MODEL)

    pw = _prep_weights(w_in, q_norm_g, kv_norm_g, w_q_up, w_kv_up, q_head_norm_g, k_head_norm_g,
                       conv_w, norm1_g, norm2_g, w_o, w_ffn_gate, w_ffn_up, w_ffn_down)
    rope_tabs = _rope_tables(ts)
    lgl, lgc = _decay_lanes(ret_decay_fwd, ret_decay_bwd)
    kpe_ctx = jnp.pad(cache_kpe, ((0, 0), (0, 0), (0, 0), (QK_NOPE, LANES - QK_HEAD)))
    kc, vc = _ctx_kv_call(cache_ckv, kpe_ctx, pw["wkv"], pw["khg"])

    xp = x_prompt.reshape(bp * tp, D_MODEL)
    xs = x_sample.reshape(bs * ts, D_MODEL)
    new_ckv = jnp.zeros((bp, DEPTH, tp, KV_LORA), F32)
    new_kpe = jnp.zeros((bp, DEPTH, tp, QK_ROPE), F32)
    new_state = jnp.zeros((bp, DEPTH, 2, RET_HEADS, RET_DK, RET_DV), F32)
    for l in range(DEPTH):
        attn, new_ckv, new_kpe, conv, rq, rk, rv, rg = _proj_call(
            xp, mods, pw, l, None, (new_ckv, new_kpe), latent=False, seq=tp, tm=TM)
        r3 = lambda a: a.reshape(bp, tp, RET_W)
        ret, new_state = _ret_call(lgl, lgc, r3(rq), r3(rk), r3(rv), r3(rg), new_state, l,
                                   latent=False, group=RET_GROUP)
        xp = _out_call(xp, attn, conv, ret.reshape(bp * tp, RET_W), mods, pw, l,
                       latent=False, seq=tp, tm=TM_OUT)

        q, k, v, conv, rq, rk, rv, rg = _proj_call(
            xs, mods, pw, l, rope_tabs, (), latent=True, seq=ts, tm=TM)
        attn = _attn_call(q.reshape(bs, ts, QK_W), k.reshape(bs, ts, QK_W),
                          v.reshape(bs, ts, MLA_W), kc, vc, l, tq=TQ, unit=TQ_UNIT)
        r3 = lambda a: a.reshape(bs, ts, RET_W)
        (ret,) = _ret_call(lgl, lgc, r3(rq), r3(rk), r3(rv), r3(rg), state_ret, l,
                           latent=True, group=RET_GROUP_LATENT)
        xs = _out_call(xs, attn.reshape(bs * ts, MLA_W), conv, ret.reshape(bs * ts, RET_W),
                       mods, pw, l, latent=True, seq=ts, tm=TM_OUT)

    return (xp.reshape(bp, tp, D_MODEL), xs.reshape(bs, ts, D_MODEL),
            new_ckv, new_kpe, new_state)
```

```python
import functools

import jax
import jax.numpy as jnp
from jax import lax
from jax.experimental import pallas as pl
from jax.experimental.pallas import tpu as pltpu

D_MODEL = 1024
DEPTH = 2
GRID_W = 64
MLA_HEADS = 8
Q_LORA = 256
KV_LORA = 128
QK_NOPE = 64
QK_ROPE = 32
V_HEAD = 64
QK_HEAD = QK_NOPE + QK_ROPE
MLA_W = MLA_HEADS * V_HEAD
CONV_W = 256
RET_HEADS = 4
RET_DK = 64
RET_DV = 64
RET_W = RET_HEADS * RET_DV
RET_CHUNK = 128
FFN_HIDDEN = 2816
ROPE_THETA = 10000.0
EPS = 1e-6
LOG2_E = 1.4426950408889634

LANES = 128
SUBLANES = 8
MXU_N = 256
HEAD_PAD = LANES
QK_W = MLA_HEADS * HEAD_PAD
ROPE_PAIR = QK_ROPE // 4
VMEM_LIMIT = 56 * 1024 * 1024

C_QLAT = 0
C_KVLAT = C_QLAT + Q_LORA
C_KPE = C_KVLAT + KV_LORA
C_GB = C_KPE + LANES
C_GC = C_GB + CONV_W
C_XIN = C_GC + CONV_W
C_RQ = C_XIN + CONV_W
C_RK = C_RQ + RET_W
C_RV = C_RK + RET_W
C_RG = C_RV + RET_W
IN_COLS_PAD = C_RG + RET_W

BF16 = jnp.bfloat16
F32 = jnp.float32


def _dot(a, b):
    return jnp.dot(a, b, preferred_element_type=F32)


def _dot_nt(a, b):
    return lax.dot_general(a, b, (((1,), (1,)), ((), ())), preferred_element_type=F32)


def _rms(x, n=None):
    n = x.shape[-1] if n is None else n
    return x * lax.rsqrt(jnp.sum(x * x, axis=-1, keepdims=True) * (1.0 / n) + EPS)


def _silu(x):
    return x * (1.0 / (1.0 + jnp.exp(-x)))


def _const_spec(shape):
    nd = len(shape)
    return pl.BlockSpec(shape, lambda *_: (0,) * nd, pipeline_mode=pl.Buffered(1))


def _layer_spec(shape, layer):
    nd = len(shape)
    return pl.BlockSpec((None,) + tuple(shape), lambda *_: (layer,) + (0,) * nd,
                        pipeline_mode=pl.Buffered(1))


def _softmax_pv(s_list, v_list):
    m = functools.reduce(jnp.maximum, [jnp.max(s, axis=-1, keepdims=True) for s in s_list])
    l, o = None, None
    for s, v in zip(s_list, v_list):
        p = jnp.exp2(s - m)
        ls = jnp.sum(p, axis=-1, keepdims=True)
        os_ = _dot(p.astype(BF16), v)
        l = ls if l is None else l + ls
        o = os_ if o is None else o + os_
    return o * (1.0 / l)


def _merge_head_pair(o0, o1):
    lane = lax.broadcasted_iota(jnp.int32, o0.shape, 1)
    return jnp.where(lane < V_HEAD, o0, o1).astype(BF16)


ADA_TN = 1536


def _ada_kernel(cond_ref, w_ref, b_ref, o_ref):
    a = _silu(cond_ref[...]).astype(BF16)
    o_ref[...] = _dot(a, w_ref[...].astype(BF16)) + b_ref[...]


def _ada_call(cond, ada_w, ada_b):
    n = 6 * D_MODEL
    return pl.pallas_call(
        _ada_kernel,
        grid=(DEPTH, n // ADA_TN),
        in_specs=[
            pl.BlockSpec((SUBLANES, D_MODEL), lambda l, j: (0, 0)),
            pl.BlockSpec((None, D_MODEL, ADA_TN), lambda l, j: (l, 0, j)),
            pl.BlockSpec((None, 1, ADA_TN), lambda l, j: (l, 0, j)),
        ],
        out_specs=pl.BlockSpec((None, SUBLANES, ADA_TN), lambda l, j: (l, 0, j)),
        out_shape=jax.ShapeDtypeStruct((DEPTH, SUBLANES, n), F32),
        compiler_params=pltpu.CompilerParams(
            dimension_semantics=("arbitrary", "arbitrary"), vmem_limit_bytes=VMEM_LIMIT),
        name="ada_mod",
    )(cond, ada_w, ada_b.reshape(DEPTH, 1, n))


def _key_slab(kn, shared, pe_ssq, g):
    ssq = jnp.sum(kn * kn, axis=-1, keepdims=True) + pe_ssq
    return ((kn * g + shared) * lax.rsqrt(ssq * (1.0 / QK_HEAD) + EPS)).astype(BF16)


def _ctx_kv_kernel(ckv_ref, kpe_ref, wkv_ref, khg_ref, k_ref, v_ref):
    kv = _dot(ckv_ref[...].astype(BF16), wkv_ref[...])
    kpe = kpe_ref[...]
    g = khg_ref[...]
    shared = kpe * g
    pe_ssq = jnp.sum(kpe * kpe, axis=-1, keepdims=True)
    for hd in range(MLA_HEADS):
        sl = slice(hd * HEAD_PAD, (hd + 1) * HEAD_PAD)
        k_ref[:, sl] = _key_slab(kv[:, sl], shared, pe_ssq, g)
    v_ref[...] = kv[:, QK_W:].astype(BF16)


def _ctx_kv_call(cache_ckv, kpe_slab, wkv, khg):
    b, _, t, _ = cache_ckv.shape
    return pl.pallas_call(
        _ctx_kv_kernel,
        grid=(DEPTH, b),
        in_specs=[
            pl.BlockSpec((None, None, t, KV_LORA), lambda l, i: (i, l, 0, 0)),
            pl.BlockSpec((None, None, t, LANES), lambda l, i: (i, l, 0, 0)),
            pl.BlockSpec((None, KV_LORA, QK_W + MLA_W), lambda l, i: (l, 0, 0)),
            pl.BlockSpec((None, 1, LANES), lambda l, i: (l, 0, 0)),
        ],
        out_specs=[
            pl.BlockSpec((None, None, t, QK_W), lambda l, i: (l, i, 0, 0)),
            pl.BlockSpec((None, None, t, MLA_W), lambda l, i: (l, i, 0, 0)),
        ],
        out_shape=[
            jax.ShapeDtypeStruct((DEPTH, b, t, QK_W), BF16),
            jax.ShapeDtypeStruct((DEPTH, b, t, MLA_W), BF16),
        ],
        compiler_params=pltpu.CompilerParams(
            dimension_semantics=("arbitrary", "arbitrary"), vmem_limit_bytes=VMEM_LIMIT),
        name="ctx_kv",
    )(cache_ckv, kpe_slab, wkv, khg)


def _store_layer_slot(ref, idx, layer, n_alias, value):
    if n_alias:
        ref[idx] = value
    else:
        for l in range(ref.shape[1]):
            ref[idx, l] = value if l == layer else jnp.zeros_like(value)


def _proj_kernel(*refs, latent, tiles_per_seq, tm, layer, n_alias):
    it = iter(refs)
    x_ref = next(it)
    if latent:
        xp_ref, xn_ref = next(it), next(it)
    mod_ref, g1_ref, win_ref, qng_ref, kvng_ref = (next(it) for _ in range(5))
    wq_ref, wkv_ref, qhg_ref, khg_ref, cw_ref = (next(it) for _ in range(5))
    if latent:
        cos_ref, slo_ref, shi_ref = next(it), next(it), next(it)
        q_out, k_out, v_out = next(it), next(it), next(it)
    else:
        for _ in range(n_alias):
            next(it)
        attn_out, ckv_out, kpe_out = next(it), next(it), next(it)
    conv_out, rq_out, rk_out, rv_out, rg_out = (next(it) for _ in range(5))

    shift = mod_ref[:, 0:D_MODEL]
    scale1 = 1.0 + mod_ref[:, D_MODEL:2 * D_MODEL]
    g1 = g1_ref[...]

    def norm_mod(x):
        return ((_rms(x) * g1) * scale1 + shift).astype(BF16)

    q_scale = QK_HEAD ** -0.5 * LOG2_E
    qg = qhg_ref[0:1, :] * q_scale
    kg = khg_ref[...]
    head_slice = lambda hd: slice(hd * HEAD_PAD, (hd + 1) * HEAD_PAD)
    n_sub = tm // PROJ_SUB
    subs = [slice(t * PROJ_SUB, (t + 1) * PROJ_SUB) for t in range(n_sub)]
    state = [dict() for _ in subs]

    def projection_stage(t):
        rows, st = subs[t], state[t]

        def proj(col, width):
            return _dot(st["h"], win_ref[:, col:col + width])

        def latents():
            st["h"] = norm_mod(x_ref[rows, :])
            st["q_lat"] = proj(C_QLAT, Q_LORA)
            st["kv_pe"] = proj(C_KVLAT, KV_LORA + LANES)

        def conv_inputs():
            st["u"] = proj(C_GC, CONV_W) * proj(C_XIN, CONV_W)
            if latent and t == 0:
                hh = norm_mod(jnp.concatenate([xp_ref[...], xn_ref[...]], axis=0))
                halo = (_dot(hh, win_ref[:, C_GC:C_GC + CONV_W])
                        * _dot(hh, win_ref[:, C_XIN:C_XIN + CONV_W]))
                i = pl.program_id(0)
                has_prev = jnp.where(i % tiles_per_seq != 0, 1.0, 0.0)
                has_next = jnp.where(i % tiles_per_seq != tiles_per_seq - 1, 1.0, 0.0)
                state[0]["u_before"] = halo[SUBLANES - 1:SUBLANES, :] * has_prev
                state[-1]["u_after"] = halo[SUBLANES:SUBLANES + 1, :] * has_next

        def queries():
            qn = (_rms(st["q_lat"]) * qng_ref[...]).astype(BF16)
            st["q"] = _dot(qn, wq_ref[:, 0:QK_W])
            if latent:
                st["q_sw"] = _dot(qn, wq_ref[:, QK_W:2 * QK_W])
                cos = cos_ref[rows, :]
                st["q_tab"] = qg * cos
                st["q_tab_sw"] = (qhg_ref[1:2, :] * q_scale) * (slo_ref[rows, :] + shi_ref[rows, :])

        def keys_values():
            ckv = _rms(st["kv_pe"][:, 0:KV_LORA]) * kvng_ref[...]
            kpe = st["kv_pe"][:, KV_LORA:]
            st["kv"] = _dot(ckv.astype(BF16), wkv_ref[...])
            shared = kpe * kg
            if latent:
                shared = (shared * cos_ref[rows, :]
                          + pltpu.roll(shared, LANES - ROPE_PAIR, axis=1) * slo_ref[rows, :]
                          + pltpu.roll(shared, ROPE_PAIR, axis=1) * shi_ref[rows, :])
            else:
                _store_layer_slot(ckv_out, t, layer, n_alias, ckv)
                _store_layer_slot(kpe_out, t, layer, n_alias, kpe.T[QK_NOPE:QK_HEAD, :])
            st["shared"] = shared
            st["pe_ssq"] = jnp.sum(kpe * kpe, axis=-1, keepdims=True)

        def conv_gate():
            st["gb"] = proj(C_GB, CONV_W)

        def ret_q():
            rq_out[rows, :] = proj(C_RQ, RET_W).astype(BF16)

        def ret_k():
            rk_out[rows, :] = (proj(C_RK, RET_W) * (RET_DK ** -0.5)).astype(BF16)

        def ret_v():
            rv_out[rows, :] = proj(C_RV, RET_W).astype(BF16)

        def ret_g():
            rg_out[rows, :] = proj(C_RG, RET_W)

        return [latents, conv_inputs, queries, keys_values, conv_gate, ret_q, ret_k, ret_v, ret_g]

    def conv_finish(t):
        st, u = state[t], state[t]["u"]
        zero = jnp.zeros((1, CONV_W), F32)
        if latent:
            u_before = st["u_before"] if t == 0 else state[t - 1]["u"][PROJ_SUB - 1:PROJ_SUB, :]
            u_after = st["u_after"] if t == n_sub - 1 else state[t + 1]["u"][0:1, :]
        else:
            u_before, u_after = zero, zero
        row = lax.broadcasted_iota(jnp.int32, (PROJ_SUB, CONV_W), 0)
        u_m1 = jnp.where(row == 0, u_before, pltpu.roll(u, 1, axis=0))
        u_p1 = jnp.where(row == PROJ_SUB - 1, u_after, pltpu.roll(u, PROJ_SUB - 1, axis=0))
        y = u_m1 * cw_ref[0:1, :] + u * cw_ref[1:2, :] + u_p1 * cw_ref[2:3, :]
        conv_out[subs[t], :] = (st["gb"] * y).astype(BF16)

    def head_stage(t):
        rows, st = subs[t], state[t]

        def query_slab(sl):
            qs = st["q"][:, sl]
            r = lax.rsqrt(jnp.sum(qs * qs, axis=-1, keepdims=True) * (1.0 / QK_HEAD) + EPS)
            if latent:
                return ((qs * st["q_tab"] + st["q_sw"][:, sl] * st["q_tab_sw"]) * r).astype(BF16)
            return (qs * qg * r).astype(BF16)

        def key_slab(sl):
            return _key_slab(st["kv"][:, sl], st["shared"], st["pe_ssq"], kg)

        if latent:
            def store_head(hd):
                sl = head_slice(hd)
                q_out[rows, sl] = query_slab(sl)
                k_out[rows, sl] = key_slab(sl)
                if hd == MLA_HEADS - 1:
                    v_out[rows, :] = st["kv"][:, QK_W:].astype(BF16)
            return [functools.partial(store_head, hd) for hd in range(MLA_HEADS)]

        ahead, pair = [], []

        def scores(hd):
            sl = head_slice(hd)
            return _dot_nt(query_slab(sl), key_slab(sl))

        def attend(hd):
            if hd == 0:
                ahead.extend(scores(j) for j in range(CTX_ATTN_LOOKAHEAD))
            s = ahead.pop(0)
            if hd + CTX_ATTN_LOOKAHEAD < MLA_HEADS:
                ahead.append(scores(hd + CTX_ATTN_LOOKAHEAD))
            hp = hd // 2
            v_pair = st["kv"][:, QK_W + hp * LANES:QK_W + (hp + 1) * LANES].astype(BF16)
            pair.append(_softmax_pv([s], [v_pair]))
            if hd % 2 == 1:
                attn_out[rows, hp * LANES:(hp + 1) * LANES] = _merge_head_pair(*pair)
                pair.clear()

        return [functools.partial(attend, hd) for hd in range(MLA_HEADS)]

    for piece in projection_stage(0):
        piece()
    for t in range(n_sub):
        heads = head_stage(t)
        nxt = projection_stage(t + 1) if t + 1 < n_sub else []
        for i in range(max(len(heads), len(nxt))):
            if i < len(nxt):
                nxt[i]()
            if i < len(heads):
                heads[i]()
            if i == PROJ_CONV_SLOT and t > 0:
                conv_finish(t - 1)
    conv_finish(n_sub - 1)


def _proj_call(x2d, mod, pw, layer, rope_tabs, cache_bufs, *, latent, seq, tm):
    rows = x2d.shape[0]
    tiles_per_seq = seq // tm
    n_tiles = rows // tm
    if latent:
        mod_map = lambda i: (layer, 1 + i // tiles_per_seq, 0, 0)
    else:
        assert seq == PROJ_SUB, "the context path runs one sequence's attention per sub-tile"
        mod_map = lambda i: (layer, 0, 0, 0)
    row_spec = lambda w: pl.BlockSpec((tm, w), lambda i: (i, 0))
    in_specs = [row_spec(D_MODEL)]
    args = [x2d]
    if latent:
        blk = tm // SUBLANES
        last = rows // SUBLANES - 1
        in_specs += [
            pl.BlockSpec((SUBLANES, D_MODEL), lambda i: (jnp.maximum(i * blk - 1, 0), 0)),
            pl.BlockSpec((SUBLANES, D_MODEL), lambda i: (jnp.minimum((i + 1) * blk, last), 0)),
        ]
        args += [x2d, x2d]
    in_specs += [
        pl.BlockSpec((None, None, 1, 6 * D_MODEL), mod_map),
        _layer_spec((1, D_MODEL), layer),
        _layer_spec((D_MODEL, IN_COLS_PAD), layer),
        _layer_spec((1, Q_LORA), layer),
        _layer_spec((1, KV_LORA), layer),
        _layer_spec((Q_LORA, 2 * QK_W), layer),
        _layer_spec((KV_LORA, QK_W + MLA_W), layer),
        _layer_spec((SUBLANES, LANES), layer),
        _layer_spec((1, LANES), layer),
        _layer_spec((SUBLANES, CONV_W), layer),
    ]
    args += [mod, pw["g1"], pw["w_in"], pw["qng"], pw["kvng"], pw["wq"], pw["wkv"],
             pw["qhg"], pw["khg"], pw["convw"]]
    if latent:
        in_specs += [pl.BlockSpec((tm, LANES), lambda i: (i % tiles_per_seq, 0))] * 3
        args += list(rope_tabs)
        outs = [(QK_W, BF16), (QK_W, BF16), (MLA_W, BF16)]
        aliases = {}
    else:
        outs = [(MLA_W, BF16)]
    out_specs = [row_spec(w) for w, _ in outs]
    out_shape = [jax.ShapeDtypeStruct((rows, w), dt) for w, dt in outs]
    n_alias = 0
    if not latent:
        n_seq = tm // seq
        batch = rows // seq
        shapes = [(batch, DEPTH, seq, KV_LORA), (batch, DEPTH, QK_ROPE, seq)]
        if cache_bufs is None:
            aliases = {}
            out_specs += [pl.BlockSpec((n_seq,) + shp[1:], lambda i: (i, 0, 0, 0)) for shp in shapes]
        else:
            n_alias = len(cache_bufs)
            aliases = {len(args) + j: len(outs) + j for j in range(n_alias)}
            in_specs += [pl.BlockSpec(memory_space=pl.ANY)] * n_alias
            args += list(cache_bufs)
            out_specs += [pl.BlockSpec((n_seq, None) + shp[2:], lambda i: (i, layer, 0, 0))
                          for shp in shapes]
        out_shape += [jax.ShapeDtypeStruct(shp, F32) for shp in shapes]
    tail = [(CONV_W, BF16), (RET_W, BF16), (RET_W, BF16), (RET_W, BF16), (RET_W, F32)]
    out_specs += [row_spec(w) for w, _ in tail]
    out_shape += [jax.ShapeDtypeStruct((rows, w), dt) for w, dt in tail]
    return pl.pallas_call(
        functools.partial(_proj_kernel, latent=latent, tiles_per_seq=tiles_per_seq, tm=tm,
                          layer=layer, n_alias=n_alias),
        grid=(n_tiles,),
        in_specs=in_specs,
        out_specs=out_specs,
        out_shape=out_shape,
        input_output_aliases=aliases,
        compiler_params=pltpu.CompilerParams(
            dimension_semantics=("arbitrary",), vmem_limit_bytes=VMEM_LIMIT),
        name="proj_latent" if latent else "proj_ctx",
    )(*args)


def _attn_kernel(q_ref, k_ref, v_ref, kc_ref, vc_ref, o_ref, *, unit):
    units = [(r0, j) for r0 in range(0, q_ref.shape[0], unit) for j in range(2)]

    def scores(r0, j):
        sl = slice(j * HEAD_PAD, (j + 1) * HEAD_PAD)
        q = q_ref[r0:r0 + unit, sl]
        return [_dot_nt(q, k_ref[:, sl]), _dot_nt(q, kc_ref[:, sl])]

    ahead = [scores(*u) for u in units[:ATTN_LOOKAHEAD]]
    pair = []
    for idx, (r0, j) in enumerate(units):
        s = ahead.pop(0)
        if idx + ATTN_LOOKAHEAD < len(units):
            ahead.append(scores(*units[idx + ATTN_LOOKAHEAD]))
        pair.append(_softmax_pv(s, [v_ref[...], vc_ref[...]]))
        if j == 1:
            o_ref[r0:r0 + unit, :] = _merge_head_pair(*pair)
            pair = []


def _attn_call(q, k, v, kc, vc, layer, *, tq, unit):
    b, t, _ = q.shape
    tk, tc = k.shape[1], kc.shape[2]
    return pl.pallas_call(
        functools.partial(_attn_kernel, unit=unit),
        grid=(b, MLA_HEADS // 2, t // tq),
        in_specs=[
            pl.BlockSpec((None, tq, 2 * HEAD_PAD), lambda bi, hp, i: (bi, i, hp)),
            pl.BlockSpec((None, tk, 2 * HEAD_PAD), lambda bi, hp, i: (bi, 0, hp)),
            pl.BlockSpec((None, tk, 2 * V_HEAD), lambda bi, hp, i: (bi, 0, hp)),
            pl.BlockSpec((None, None, tc, 2 * HEAD_PAD), lambda bi, hp, i: (layer, bi, 0, hp)),
            pl.BlockSpec((None, None, tc, 2 * V_HEAD), lambda bi, hp, i: (layer, bi, 0, hp)),
        ],
        out_specs=pl.BlockSpec((None, tq, 2 * V_HEAD), lambda bi, hp, i: (bi, i, hp)),
        out_shape=jax.ShapeDtypeStruct((b, t, MLA_W), BF16),
        compiler_params=pltpu.CompilerParams(
            dimension_semantics=("arbitrary", "arbitrary", "arbitrary"),
            vmem_limit_bytes=VMEM_LIMIT),
        name="attn_latent",
    )(q, k, v, kc, vc)


def _ret_kernel(*refs, latent, n_chunks, group, layer, n_alias):
    if latent:
        lgl_ref, lgc_ref, rq_ref, rk_ref, rv_ref, rg_ref, s0_ref, out_ref, o_acc, st_ref = refs
    else:
        lgl_ref, lgc_ref, rq_ref, rk_ref, rv_ref, rg_ref = refs[:6]
        out_ref, sfin_ref, o_acc, st_ref = refs[6 + n_alias:]
    c = RET_CHUNK
    seqs = range(group)
    lg_f, lg_b = lgl_ref[0:1, :], lgl_ref[1:2, :]
    pos = lax.broadcasted_iota(jnp.int32, (c, RET_W), 0).astype(F32)
    qdec_f = jnp.exp(lg_f * (pos + 1.0)).astype(BF16)
    kdec_f = jnp.exp(lg_f * (c - 1.0 - pos)).astype(BF16)
    qdec_b = jnp.exp(lg_b * (c - pos)).astype(BF16)
    kdec_b = jnp.exp(lg_b * pos).astype(BF16)
    cdec_f = jnp.exp(lg_f * float(c))
    cdec_b = jnp.exp(lg_b * float(c))
    ri = lax.broadcasted_iota(jnp.int32, (c, RET_HEADS * c), 0)
    ci = jnp.bitwise_and(lax.broadcasted_iota(jnp.int32, (c, RET_HEADS * c), 1), c - 1)
    diff = (ri - ci).astype(F32)
    dcat = (jnp.where(diff >= 0, jnp.exp(lgc_ref[0:1, :] * diff), 0.0)
            + jnp.where(diff <= 0, jnp.exp(lgc_ref[1:2, :] * (-diff)), 0.0))
    head_shift = RET_DV.bit_length() - 1
    lane_head = jnp.right_shift(lax.broadcasted_iota(jnp.int32, (c, RET_W), 1), head_shift)
    r_head = jnp.right_shift(lax.broadcasted_iota(jnp.int32, (RET_W, RET_W), 0), head_shift)
    c_head = jnp.right_shift(lax.broadcasted_iota(jnp.int32, (RET_W, RET_W), 1), head_shift)
    same_head = r_head == c_head
    head_mean = jnp.where(same_head, 1.0 / RET_DV, 0.0).astype(BF16)

    def stack_heads(a):
        zero = jnp.zeros_like(a)
        return jnp.concatenate([jnp.where(lane_head == hd, a, zero) for hd in range(RET_HEADS)], axis=0)

    def chunk(n):
        return n * c if isinstance(n, int) else pl.multiple_of(n * c, c)

    def load(sl):
        return ([rq_ref[g, sl, :] for g in seqs], [rk_ref[g, sl, :] for g in seqs],
                [rv_ref[g, sl, :] for g in seqs])

    def state_terms(d, qs, ks, vs, qdec, kdec):
        inc = [_dot((k * kdec).T, v) for k, v in zip(ks, vs)]
        zero = jnp.zeros((RET_W, RET_W), BF16)
        inter = [_dot(q * qdec, jnp.where(same_head, st_ref[g, d].astype(BF16), zero))
                 for g, q in zip(seqs, qs)]
        return inter, inc

    def state_update(d, inc, cdec):
        for g in seqs:
            st_ref[g, d] = st_ref[g, d] * cdec + inc[g]

    def finish(sl, os_):
        sq = [o * o for o in os_]
        hi = [x.astype(BF16) for x in sq]
        ms = [_dot(h, head_mean) + _dot((x - h.astype(F32)).astype(BF16), head_mean)
              for x, h in zip(sq, hi)]
        for g in seqs:
            y = os_[g] * lax.rsqrt(ms[g] + EPS)
            out_ref[g, sl, :] = (y * _silu(rg_ref[g, sl, :])).astype(BF16)

    def step(j, second_half):
        sl_f = pl.ds(chunk(j), c)
        sl_b = pl.ds(chunk(n_chunks - 1 - j), c)
        qf, kf, vf = load(sl_f)
        qb, kb, vb = load(sl_b)
        scores = [_dot_nt(q, stack_heads(k)) for q, k in zip(qf, kf)]
        inter_f, inc_f = state_terms(0, qf, kf, vf, qdec_f, kdec_f)
        inter_b, inc_b = state_terms(1, qb, kb, vb, qdec_b, kdec_b)
        o_f = [_dot((scores[g] * dcat).astype(BF16), stack_heads(vf[g])) + inter_f[g] for g in seqs]
        state_update(0, inc_f, cdec_f)
        state_update(1, inc_b, cdec_b)
        if second_half:
            finish(sl_f, [o_acc[g, sl_f, :] + o_f[g] for g in seqs])
            finish(sl_b, [o_acc[g, sl_b, :] + inter_b[g] for g in seqs])
        else:
            for g in seqs:
                o_acc[g, sl_f, :] = o_f[g]
                o_acc[g, sl_b, :] = inter_b[g]

    def scan(lo, hi, second_half):
        if hi - lo <= 1:
            for j in range(lo, hi):
                step(j, second_half)
        else:
            lax.fori_loop(lo, hi, lambda j, carry: (step(j, second_half), carry)[1], 0)

    st_ref[...] = jnp.zeros_like(st_ref)
    if latent:
        for g in seqs:
            for d in range(2):
                for hd in range(RET_HEADS):
                    st_ref[g, d, hd * RET_DK:(hd + 1) * RET_DK, hd * RET_DV:(hd + 1) * RET_DV] = s0_ref[g, d, hd]
    half = n_chunks // 2
    scan(0, half, False)
    scan(half, n_chunks, True)
    if not latent:
        for g in seqs:
            for d in range(2):
                for hd in range(RET_HEADS):
                    block = st_ref[g, d, hd * RET_DK:(hd + 1) * RET_DK, hd * RET_DV:(hd + 1) * RET_DV]
                    if n_alias:
                        sfin_ref[g, d, hd] = block
                    else:
                        for l in range(sfin_ref.shape[1]):
                            sfin_ref[g, l, d, hd] = block if l == layer else jnp.zeros_like(block)


def _ret_call(lgl, lgc, rq, rk, rv, rg, state_ret, layer, *, latent, group):
    b, t, _ = rq.shape
    assert (t // RET_CHUNK) % 2 == 0, "the two scans meet in the middle of an even chunk count"
    seq_spec = pl.BlockSpec((group, t, RET_W), lambda i: (i, 0, 0))
    st_block = (2, RET_HEADS, RET_DK, RET_DV)
    in_specs = [_layer_spec((SUBLANES, RET_W), layer),
                _layer_spec((SUBLANES, RET_HEADS * RET_CHUNK), layer),
                seq_spec, seq_spec, seq_spec, seq_spec]
    args = [lgl, lgc, rq, rk, rv, rg]
    out_specs = [seq_spec]
    out_shape = [jax.ShapeDtypeStruct((b, t, RET_W), BF16)]
    st_spec = pl.BlockSpec((group, None) + st_block, lambda i: (i, layer, 0, 0, 0, 0))
    aliases, n_alias = {}, 0
    if latent:
        in_specs.append(st_spec)
        args.append(state_ret)
    elif state_ret is None:
        out_specs.append(pl.BlockSpec((group, DEPTH) + st_block, lambda i: (i, 0, 0, 0, 0, 0)))
        out_shape.append(jax.ShapeDtypeStruct((b, DEPTH) + st_block, F32))
    else:
        in_specs.append(pl.BlockSpec(memory_space=pl.ANY))
        aliases, n_alias = {len(args): 1}, 1
        args.append(state_ret)
        out_specs.append(st_spec)
        out_shape.append(jax.ShapeDtypeStruct(state_ret.shape, state_ret.dtype))
    return pl.pallas_call(
        functools.partial(_ret_kernel, latent=latent, n_chunks=t // RET_CHUNK, group=group,
                          layer=layer, n_alias=n_alias),
        grid=(b // group,),
        in_specs=in_specs,
        out_specs=out_specs,
        out_shape=out_shape,
        input_output_aliases=aliases,
        scratch_shapes=[pltpu.VMEM((group, t, RET_W), F32),
                        pltpu.VMEM((group, 2, RET_W, RET_W), F32)],
        compiler_params=pltpu.CompilerParams(
            dimension_semantics=("arbitrary",), vmem_limit_bytes=VMEM_LIMIT),
        name="ret_latent" if latent else "ret_ctx",
    )(*args)


def _out_kernel(x_ref, attn_ref, conv_ref, ret_ref, mod_ref, g2_ref, wo_ref, wg_ref, wu_ref,
                wd_ref, o_ref, act_ref):
    subs = [slice(r0, r0 + OUT_SUB) for r0 in range(0, x_ref.shape[0], OUT_SUB)]
    x1 = []
    for rows in subs:
        mix = (_dot(attn_ref[rows, :], wo_ref[0:MLA_W, :])
               + _dot(conv_ref[rows, :], wo_ref[MLA_W:MLA_W + CONV_W, :])
               + _dot(ret_ref[rows, :], wo_ref[MLA_W + CONV_W:, :]))
        x1.append(x_ref[rows, :] + mod_ref[:, 2 * D_MODEL:3 * D_MODEL] * mix)
    for rows, xr in zip(subs, x1):
        h = ((_rms(xr) * g2_ref[...]) * (1.0 + mod_ref[:, 4 * D_MODEL:5 * D_MODEL])
             + mod_ref[:, 3 * D_MODEL:4 * D_MODEL]).astype(BF16)
        for j in range(FFN_HIDDEN // MXU_N):
            sl = slice(j * MXU_N, (j + 1) * MXU_N)
            gate = _dot(h, wg_ref[:, sl])
            up = _dot(h, wu_ref[:, sl])
            act_ref[rows, sl] = (_silu(gate) * up).astype(BF16)
        ffn = _dot(act_ref[rows, :], wd_ref[...])
        o_ref[rows, :] = xr + mod_ref[:, 5 * D_MODEL:6 * D_MODEL] * ffn


def _out_call(x2d, attn, conv, ret, mod, pw, layer, *, latent, seq, tm):
    rows = x2d.shape[0]
    tiles_per_seq = seq // tm
    if latent:
        mod_map = lambda i: (layer, 1 + i // tiles_per_seq, 0, 0)
    else:
        mod_map = lambda i: (layer, 0, 0, 0)
    row_spec = lambda w: pl.BlockSpec((tm, w), lambda i: (i, 0))
    return pl.pallas_call(
        _out_kernel,
        grid=(rows // tm,),
        in_specs=[
            row_spec(D_MODEL), row_spec(MLA_W), row_spec(CONV_W), row_spec(RET_W),
            pl.BlockSpec((None, None, 1, 6 * D_MODEL), mod_map),
            _layer_spec((1, D_MODEL), layer),
            _layer_spec((D_MODEL, D_MODEL), layer),
            _layer_spec((D_MODEL, FFN_HIDDEN), layer),
            _layer_spec((D_MODEL, FFN_HIDDEN), layer),
            _layer_spec((FFN_HIDDEN, D_MODEL), layer),
        ],
        out_specs=row_spec(D_MODEL),
        out_shape=jax.ShapeDtypeStruct((rows, D_MODEL), F32),
        scratch_shapes=[pltpu.VMEM((tm, FFN_HIDDEN), BF16)],
        compiler_params=pltpu.CompilerParams(
            dimension_semantics=("arbitrary",), vmem_limit_bytes=VMEM_LIMIT),
        name="out_latent" if latent else "out_ctx",
    )(x2d, attn, conv, ret, mod, pw["g2"], pw["wo"], pw["wg"], pw["wu"], pw["wd"])


def _pad_heads(w, heads, width):
    lead = w.shape[:-1]
    w = w.reshape(*lead, heads, width)
    w = jnp.pad(w, [(0, 0)] * len(lead) + [(0, 0), (0, HEAD_PAD - width)])
    return w.reshape(*lead, heads * HEAD_PAD)


def _swap_rope_pairs(w):
    nope = jnp.zeros_like(w[..., :QK_NOPE])
    rope = w[..., QK_NOPE:].reshape(*w.shape[:-1], 2, 2, ROPE_PAIR)
    rope = rope[..., ::-1, :].reshape(*w.shape[:-1], QK_ROPE)
    return jnp.concatenate([nope, rope], axis=-1)


def _w_in_layout_kernel(wt_ref, o_ref):
    k = wt_ref.shape[1]

    def put(dst, rows_t):
        o_ref[:, dst:dst + rows_t.shape[0]] = rows_t.T.astype(BF16)

    for c0 in range(0, C_KPE, LANES):
        put(c0, wt_ref[c0:c0 + LANES, :])
    put(C_KPE, jnp.concatenate([jnp.zeros((QK_NOPE, k), F32), wt_ref[C_KPE:C_KPE + QK_ROPE, :],
                                jnp.zeros((LANES - QK_HEAD, k), F32)], axis=0))
    shift = C_GB - (C_KPE + QK_ROPE)
    for c0 in range(C_GB, IN_COLS_PAD, MXU_N):
        put(c0, wt_ref[c0 - shift:c0 - shift + MXU_N, :])


def _w_in_layout_call(w_in):
    d, k, n = w_in.shape
    return pl.pallas_call(
        _w_in_layout_kernel,
        grid=(d,),
        in_specs=[pl.BlockSpec((None, n, k), lambda l: (l, 0, 0))],
        out_specs=pl.BlockSpec((None, k, IN_COLS_PAD), lambda l: (l, 0, 0)),
        out_shape=jax.ShapeDtypeStruct((d, k, IN_COLS_PAD), BF16),
        compiler_params=pltpu.CompilerParams(
            dimension_semantics=("arbitrary",), vmem_limit_bytes=VMEM_LIMIT),
        name="w_in_layout",
    )(jnp.swapaxes(w_in, 1, 2))


def _prep_weights(w_in, q_norm_g, kv_norm_g, w_q_up, w_kv_up, q_head_norm_g, k_head_norm_g,
                  conv_w, norm1_g, norm2_g, w_o, w_ffn_gate, w_ffn_up, w_ffn_down):
    d = DEPTH
    wq_h = w_q_up.reshape(d, Q_LORA, MLA_HEADS, QK_HEAD)
    wq = jnp.concatenate([_pad_heads(wq_h.reshape(d, Q_LORA, -1), MLA_HEADS, QK_HEAD),
                          _pad_heads(_swap_rope_pairs(wq_h).reshape(d, Q_LORA, -1), MLA_HEADS, QK_HEAD)],
                         axis=-1)
    kvu = w_kv_up.reshape(d, KV_LORA, MLA_HEADS, QK_NOPE + V_HEAD)
    wk = _pad_heads(kvu[..., :QK_NOPE].reshape(d, KV_LORA, -1), MLA_HEADS, QK_NOPE)
    wv = kvu[..., QK_NOPE:].reshape(d, KV_LORA, MLA_W)
    lane_pad = ((0, 0), (0, HEAD_PAD - QK_HEAD))
    qhg = jnp.stack([jnp.pad(q_head_norm_g, lane_pad),
                     jnp.pad(_swap_rope_pairs(q_head_norm_g), lane_pad)], axis=1)
    return {
        "g1": norm1_g.reshape(d, 1, D_MODEL),
        "g2": norm2_g.reshape(d, 1, D_MODEL),
        "w_in": _w_in_layout_call(w_in),
        "qng": q_norm_g.reshape(d, 1, Q_LORA),
        "kvng": kv_norm_g.reshape(d, 1, KV_LORA),
        "wq": wq.astype(BF16),
        "wkv": jnp.concatenate([wk, wv], axis=-1).astype(BF16),
        "qhg": jnp.pad(qhg, ((0, 0), (0, SUBLANES - 2), (0, 0))),
        "khg": jnp.pad(k_head_norm_g, lane_pad).reshape(d, 1, LANES),
        "convw": jnp.pad(conv_w, ((0, 0), (0, SUBLANES - 3), (0, 0))),
        "wo": w_o.astype(BF16),
        "wg": w_ffn_gate.astype(BF16),
        "wu": w_ffn_up.astype(BF16),
        "wd": w_ffn_down.astype(BF16),
    }


def _rope_tables(t):
    half = QK_ROPE // 2
    freqs = jnp.power(ROPE_THETA, -jnp.arange(0, half, 2, dtype=F32) / half)
    lane = jnp.arange(LANES)
    rel = lane - QK_NOPE
    in_rope = (rel >= 0) & (rel < QK_ROPE)
    freq_lane = freqs[jnp.clip(rel, 0, QK_ROPE - 1) % ROPE_PAIR]
    tok = jnp.arange(t)
    row = (tok // GRID_W).astype(F32)
    col = (tok % GRID_W).astype(F32)
    pos = jnp.where((rel < half)[None, :], row[:, None], col[:, None])
    ang = pos * freq_lane[None, :]
    first = in_rope & ((rel % half) < ROPE_PAIR)
    second = in_rope & ((rel % half) >= ROPE_PAIR)
    cos = jnp.where(in_rope[None, :], jnp.cos(ang), 1.0)
    sin = jnp.sin(ang)
    sin_lo = jnp.where(first[None, :], -sin, 0.0)
    sin_hi = jnp.where(second[None, :], sin, 0.0)
    return cos, sin_lo, sin_hi


def _decay_lanes(ret_decay_fwd, ret_decay_bwd):
    lg = jnp.stack([jax.nn.log_sigmoid(ret_decay_fwd.astype(F32)),
                    jax.nn.log_sigmoid(ret_decay_bwd.astype(F32))], axis=1)
    pad = ((0, 0), (0, SUBLANES - 2), (0, 0))
    return (jnp.pad(jnp.repeat(lg, RET_DV, axis=2), pad),
            jnp.pad(jnp.repeat(lg, RET_CHUNK, axis=2), pad))


TM = 1024
PROJ_SUB = 256
PROJ_CONV_SLOT = 1
TM_OUT = 1024
OUT_SUB = 256
RET_GROUP = 4
RET_GROUP_LATENT = 2
TQ = 2048
TQ_UNIT = 128
CTX_ATTN_LOOKAHEAD = 2
ATTN_LOOKAHEAD = 1


def kernel(x_prompt, x_sample, cache_ckv, cache_kpe, state_ret, c, c_ctx, ada_w, ada_b, norm1_g, norm2_g, w_in, q_norm_g, kv_norm_g, w_q_up, w_kv_up, q_head_norm_g, k_head_norm_g, conv_w, ret_decay_fwd, ret_decay_bwd, w_o, w_ffn_gate, w_ffn_up, w_ffn_down):
    bp, tp, _ = x_prompt.shape
    bs, ts, _ = x_sample.shape

    cond = jnp.concatenate([c_ctx[None, :], c, jnp.zeros((SUBLANES - 1 - bs, D_MODEL), F32)], axis=0)
    mods = _ada_call(cond, ada_w, ada_b).reshape(DEPTH, SUBLANES, 1, 6 * D_MODEL)

    pw = _prep_weights(w_in, q_norm_g, kv_norm_g, w_q_up, w_kv_up, q_head_norm_g, k_head_norm_g,
                       conv_w, norm1_g, norm2_g, w_o, w_ffn_gate, w_ffn_up, w_ffn_down)
    rope_tabs = _rope_tables(ts)
    lgl, lgc = _decay_lanes(ret_decay_fwd, ret_decay_bwd)
    kpe_ctx = jnp.pad(cache_kpe, ((0, 0), (0, 0), (0, 0), (QK_NOPE, LANES - QK_HEAD)))
    kc, vc = _ctx_kv_call(cache_ckv, kpe_ctx, pw["wkv"], pw["khg"])

    xp = x_prompt.reshape(bp * tp, D_MODEL)
    xs = x_sample.reshape(bs * ts, D_MODEL)
    caches, new_state = None, None
    for l in range(DEPTH):
        attn, *caches, conv, rq, rk, rv, rg = _proj_call(
            xp, mods, pw, l, None, caches, latent=False, seq=tp, tm=TM)
        r3 = lambda a: a.reshape(bp, tp, RET_W)
        ret, new_state = _ret_call(lgl, lgc, r3(rq), r3(rk), r3(rv), r3(rg), new_state, l,
                                   latent=False, group=RET_GROUP)
        xp = _out_call(xp, attn, conv, ret.reshape(bp * tp, RET_W), mods, pw, l,
                       latent=False, seq=tp, tm=TM_OUT)

        q, k, v, conv, rq, rk, rv, rg = _proj_call(
            xs, mods, pw, l, rope_tabs, (), latent=True, seq=ts, tm=TM)
        attn = _attn_call(q.reshape(bs, ts, QK_W), k.reshape(bs, ts, QK_W),
                          v.reshape(bs, ts, MLA_W), kc, vc, l, tq=TQ, unit=TQ_UNIT)
        r3 = lambda a: a.reshape(bs, ts, RET_W)
        (ret,) = _ret_call(lgl, lgc, r3(rq), r3(rk), r3(rv), r3(rg), state_ret, l,
                           latent=True, group=RET_GROUP_LATENT)
        xs = _out_call(xs, attn.reshape(bs * ts, MLA_W), conv, ret.reshape(bs * ts, RET_W),
                       mods, pw, l, latent=True, seq=ts, tm=TM_OUT)

    return (xp.reshape(bp, tp, D_MODEL), xs.reshape(bs, ts, D_MODEL),
            caches[0], jnp.swapaxes(caches[1], 2, 3), new_state)
```

```python
import functools

import jax
import jax.numpy as jnp
from jax import lax
from jax.experimental import pallas as pl
from jax.experimental.pallas import tpu as pltpu

D_MODEL = 1024
DEPTH = 2
GRID_W = 64
MLA_HEADS = 8
Q_LORA = 256
KV_LORA = 128
QK_NOPE = 64
QK_ROPE = 32
V_HEAD = 64
QK_HEAD = QK_NOPE + QK_ROPE
MLA_W = MLA_HEADS * V_HEAD
CONV_W = 256
RET_HEADS = 4
RET_DK = 64
RET_DV = 64
RET_W = RET_HEADS * RET_DV
RET_CHUNK = 128
FFN_HIDDEN = 2816
ROPE_THETA = 10000.0
EPS = 1e-6
LOG2_E = 1.4426950408889634

LANES = 128
SUBLANES = 8
MXU_N = 256
HEAD_PAD = LANES
QK_W = MLA_HEADS * HEAD_PAD
ROPE_PAIR = QK_ROPE // 4
VMEM_LIMIT = 56 * 1024 * 1024

C_QLAT = 0
C_KVLAT = C_QLAT + Q_LORA
C_KPE = C_KVLAT + KV_LORA
C_GB = C_KPE + LANES
C_GC = C_GB + CONV_W
C_XIN = C_GC + CONV_W
C_RQ = C_XIN + CONV_W
C_RK = C_RQ + RET_W
C_RV = C_RK + RET_W
C_RG = C_RV + RET_W
IN_COLS_PAD = C_RG + RET_W

BF16 = jnp.bfloat16
F32 = jnp.float32


def _dot(a, b):
    return jnp.dot(a, b, preferred_element_type=F32)


def _dot_nt(a, b):
    return lax.dot_general(a, b, (((1,), (1,)), ((), ())), preferred_element_type=F32)


def _rms(x, n=None):
    n = x.shape[-1] if n is None else n
    return x * lax.rsqrt(jnp.sum(x * x, axis=-1, keepdims=True) * (1.0 / n) + EPS)


def _silu(x):
    return x * (1.0 / (1.0 + jnp.exp(-x)))


def _const_spec(shape):
    nd = len(shape)
    return pl.BlockSpec(shape, lambda *_: (0,) * nd, pipeline_mode=pl.Buffered(1))


def _layer_spec(shape, layer):
    nd = len(shape)
    return pl.BlockSpec((None,) + tuple(shape), lambda *_: (layer,) + (0,) * nd,
                        pipeline_mode=pl.Buffered(1))


def _softmax_pv(s_list, v_list):
    m = functools.reduce(jnp.maximum, [jnp.max(s, axis=-1, keepdims=True) for s in s_list])
    l, o = None, None
    for s, v in zip(s_list, v_list):
        p = jnp.exp2(s - m)
        ls = jnp.sum(p, axis=-1, keepdims=True)
        os_ = _dot(p.astype(BF16), v)
        l = ls if l is None else l + ls
        o = os_ if o is None else o + os_
    return o * (1.0 / l)


def _merge_head_pair(o0, o1):
    lane = lax.broadcasted_iota(jnp.int32, o0.shape, 1)
    return jnp.where(lane < V_HEAD, o0, o1).astype(BF16)


ADA_TN = 3072


def _ada_kernel(cond_ref, w_ref, b_ref, o_ref):
    a = _silu(cond_ref[...]).astype(BF16)
    o_ref[...] = _dot(a, w_ref[...].astype(BF16)) + b_ref[...]


def _ada_call(cond, ada_w, ada_b):
    n = 6 * D_MODEL
    return pl.pallas_call(
        _ada_kernel,
        grid=(DEPTH, n // ADA_TN),
        in_specs=[
            pl.BlockSpec((SUBLANES, D_MODEL), lambda l, j: (0, 0)),
            pl.BlockSpec((None, D_MODEL, ADA_TN), lambda l, j: (l, 0, j)),
            pl.BlockSpec((None, 1, ADA_TN), lambda l, j: (l, 0, j)),
        ],
        out_specs=pl.BlockSpec((None, SUBLANES, ADA_TN), lambda l, j: (l, 0, j)),
        out_shape=jax.ShapeDtypeStruct((DEPTH, SUBLANES, n), F32),
        compiler_params=pltpu.CompilerParams(
            dimension_semantics=("arbitrary", "arbitrary"), vmem_limit_bytes=VMEM_LIMIT),
        name="ada_mod",
    )(cond, ada_w, ada_b.reshape(DEPTH, 1, n))


def _key_slab(kn, shared, pe_ssq, g):
    ssq = jnp.sum(kn * kn, axis=-1, keepdims=True) + pe_ssq
    return ((kn * g + shared) * lax.rsqrt(ssq * (1.0 / QK_HEAD) + EPS)).astype(BF16)


def _ctx_kv_kernel(ckv_ref, kpe_ref, wkv_ref, khg_ref, k_ref, v_ref):
    g = khg_ref[...]
    for i in range(ckv_ref.shape[0]):
        kv = _dot(ckv_ref[i].astype(BF16), wkv_ref[...])
        kpe = kpe_ref[i]
        shared = kpe * g
        pe_ssq = jnp.sum(kpe * kpe, axis=-1, keepdims=True)
        for hd in range(MLA_HEADS):
            sl = slice(hd * HEAD_PAD, (hd + 1) * HEAD_PAD)
            k_ref[i, :, sl] = _key_slab(kv[:, sl], shared, pe_ssq, g)
        v_ref[i] = kv[:, QK_W:].astype(BF16)


def _ctx_kv_call(cache_ckv, kpe_slab, wkv, khg):
    b, _, t, _ = cache_ckv.shape
    return pl.pallas_call(
        _ctx_kv_kernel,
        grid=(DEPTH,),
        in_specs=[
            pl.BlockSpec((b, None, t, KV_LORA), lambda l: (0, l, 0, 0)),
            pl.BlockSpec((b, None, t, LANES), lambda l: (0, l, 0, 0)),
            pl.BlockSpec((None, KV_LORA, QK_W + MLA_W), lambda l: (l, 0, 0)),
            pl.BlockSpec((None, 1, LANES), lambda l: (l, 0, 0)),
        ],
        out_specs=[
            pl.BlockSpec((None, b, t, QK_W), lambda l: (l, 0, 0, 0)),
            pl.BlockSpec((None, b, t, MLA_W), lambda l: (l, 0, 0, 0)),
        ],
        out_shape=[
            jax.ShapeDtypeStruct((DEPTH, b, t, QK_W), BF16),
            jax.ShapeDtypeStruct((DEPTH, b, t, MLA_W), BF16),
        ],
        compiler_params=pltpu.CompilerParams(
            dimension_semantics=("arbitrary",), vmem_limit_bytes=VMEM_LIMIT),
        name="ctx_kv",
    )(cache_ckv, kpe_slab, wkv, khg)


def _store_layer_slot(ref, idx, layer, n_alias, value):
    if n_alias:
        ref[idx] = value
    else:
        for l in range(ref.shape[1]):
            ref[idx, l] = value if l == layer else jnp.zeros_like(value)


def _proj_kernel(*refs, latent, tiles_per_seq, tm, layer, n_alias):
    it = iter(refs)
    x_ref = next(it)
    if latent:
        xp_ref, xn_ref = next(it), next(it)
    mod_ref, g1_ref, win_ref, qng_ref, kvng_ref = (next(it) for _ in range(5))
    wq_ref, wkv_ref, qhg_ref, khg_ref, cw_ref = (next(it) for _ in range(5))
    if latent:
        cos_ref, slo_ref, shi_ref = next(it), next(it), next(it)
        q_out, k_out, v_out = next(it), next(it), next(it)
    else:
        for _ in range(n_alias):
            next(it)
        attn_out, ckv_out, kpe_out = next(it), next(it), next(it)
    conv_out, rq_out, rk_out, rv_out, rg_out = (next(it) for _ in range(5))

    shift = mod_ref[:, 0:D_MODEL]
    scale1 = 1.0 + mod_ref[:, D_MODEL:2 * D_MODEL]
    g1 = g1_ref[...]

    def norm_mod(x):
        return ((_rms(x) * g1) * scale1 + shift).astype(BF16)

    q_scale = QK_HEAD ** -0.5 * LOG2_E
    qg = qhg_ref[0:1, :] * q_scale
    kg = khg_ref[...]
    head_slice = lambda hd: slice(hd * HEAD_PAD, (hd + 1) * HEAD_PAD)
    n_sub = tm // PROJ_SUB
    subs = [slice(t * PROJ_SUB, (t + 1) * PROJ_SUB) for t in range(n_sub)]
    state = [dict() for _ in subs]

    def projection_stage(t):
        rows, st = subs[t], state[t]

        def proj(col, width):
            return _dot(st["h"], win_ref[:, col:col + width])

        def latents():
            st["h"] = norm_mod(x_ref[rows, :])
            st["q_lat"] = proj(C_QLAT, Q_LORA)
            st["kv_pe"] = proj(C_KVLAT, KV_LORA + LANES)

        def conv_inputs():
            st["u"] = proj(C_GC, CONV_W) * proj(C_XIN, CONV_W)
            if latent and t == 0:
                hh = norm_mod(jnp.concatenate([xp_ref[...], xn_ref[...]], axis=0))
                halo = (_dot(hh, win_ref[:, C_GC:C_GC + CONV_W])
                        * _dot(hh, win_ref[:, C_XIN:C_XIN + CONV_W]))
                i = pl.program_id(0)
                has_prev = jnp.where(i % tiles_per_seq != 0, 1.0, 0.0)
                has_next = jnp.where(i % tiles_per_seq != tiles_per_seq - 1, 1.0, 0.0)
                state[0]["u_before"] = halo[SUBLANES - 1:SUBLANES, :] * has_prev
                state[-1]["u_after"] = halo[SUBLANES:SUBLANES + 1, :] * has_next

        def queries():
            qn = (_rms(st["q_lat"]) * qng_ref[...]).astype(BF16)
            st["q"] = _dot(qn, wq_ref[:, 0:QK_W])
            if latent:
                st["q_sw"] = _dot(qn, wq_ref[:, QK_W:2 * QK_W])
                cos = cos_ref[rows, :]
                st["q_tab"] = qg * cos
                st["q_tab_sw"] = (qhg_ref[1:2, :] * q_scale) * (slo_ref[rows, :] + shi_ref[rows, :])

        def keys_values():
            ckv = _rms(st["kv_pe"][:, 0:KV_LORA]) * kvng_ref[...]
            kpe = st["kv_pe"][:, KV_LORA:]
            st["kv"] = _dot(ckv.astype(BF16), wkv_ref[...])
            shared = kpe * kg
            if latent:
                shared = (shared * cos_ref[rows, :]
                          + pltpu.roll(shared, LANES - ROPE_PAIR, axis=1) * slo_ref[rows, :]
                          + pltpu.roll(shared, ROPE_PAIR, axis=1) * shi_ref[rows, :])
            else:
                _store_layer_slot(ckv_out, t, layer, n_alias, ckv)
                _store_layer_slot(kpe_out, t, layer, n_alias, kpe.T[QK_NOPE:QK_HEAD, :])
            st["shared"] = shared
            st["pe_ssq"] = jnp.sum(kpe * kpe, axis=-1, keepdims=True)

        def conv_gate():
            st["gb"] = proj(C_GB, CONV_W)

        def ret_q():
            rq_out[rows, :] = proj(C_RQ, RET_W).astype(BF16)

        def ret_k():
            rk_out[rows, :] = (proj(C_RK, RET_W) * (RET_DK ** -0.5)).astype(BF16)

        def ret_v():
            rv_out[rows, :] = proj(C_RV, RET_W).astype(BF16)

        def ret_g():
            rg_out[rows, :] = proj(C_RG, RET_W)

        return [latents, conv_inputs, queries, keys_values, conv_gate, ret_q, ret_k, ret_v, ret_g]

    def conv_finish(t):
        st, u = state[t], state[t]["u"]
        zero = jnp.zeros((1, CONV_W), F32)
        if latent:
            u_before = st["u_before"] if t == 0 else state[t - 1]["u"][PROJ_SUB - 1:PROJ_SUB, :]
            u_after = st["u_after"] if t == n_sub - 1 else state[t + 1]["u"][0:1, :]
        else:
            u_before, u_after = zero, zero
        row = lax.broadcasted_iota(jnp.int32, (PROJ_SUB, CONV_W), 0)
        u_m1 = jnp.where(row == 0, u_before, pltpu.roll(u, 1, axis=0))
        u_p1 = jnp.where(row == PROJ_SUB - 1, u_after, pltpu.roll(u, PROJ_SUB - 1, axis=0))
        y = u_m1 * cw_ref[0:1, :] + u * cw_ref[1:2, :] + u_p1 * cw_ref[2:3, :]
        conv_out[subs[t], :] = (st["gb"] * y).astype(BF16)

    def head_stage(t):
        rows, st = subs[t], state[t]

        def query_slab(sl):
            qs = st["q"][:, sl]
            r = lax.rsqrt(jnp.sum(qs * qs, axis=-1, keepdims=True) * (1.0 / QK_HEAD) + EPS)
            if latent:
                return ((qs * st["q_tab"] + st["q_sw"][:, sl] * st["q_tab_sw"]) * r).astype(BF16)
            return (qs * qg * r).astype(BF16)

        def key_slab(sl):
            return _key_slab(st["kv"][:, sl], st["shared"], st["pe_ssq"], kg)

        if latent:
            def store_head(hd):
                sl = head_slice(hd)
                q_out[rows, sl] = query_slab(sl)
                k_out[rows, sl] = key_slab(sl)
                if hd == MLA_HEADS - 1:
                    v_out[rows, :] = st["kv"][:, QK_W:].astype(BF16)
            return [functools.partial(store_head, hd) for hd in range(MLA_HEADS)]

        ahead, pair = [], []

        def scores(hd):
            sl = head_slice(hd)
            return _dot_nt(query_slab(sl), key_slab(sl))

        def attend(hd):
            if hd == 0:
                ahead.extend(scores(j) for j in range(CTX_ATTN_LOOKAHEAD))
            s = ahead.pop(0)
            if hd + CTX_ATTN_LOOKAHEAD < MLA_HEADS:
                ahead.append(scores(hd + CTX_ATTN_LOOKAHEAD))
            hp = hd // 2
            v_pair = st["kv"][:, QK_W + hp * LANES:QK_W + (hp + 1) * LANES].astype(BF16)
            pair.append(_softmax_pv([s], [v_pair]))
            if hd % 2 == 1:
                attn_out[rows, hp * LANES:(hp + 1) * LANES] = _merge_head_pair(*pair)
                pair.clear()

        return [functools.partial(attend, hd) for hd in range(MLA_HEADS)]

    for piece in projection_stage(0):
        piece()
    for t in range(n_sub):
        heads = head_stage(t)
        nxt = projection_stage(t + 1) if t + 1 < n_sub else []
        for i in range(max(len(heads), len(nxt))):
            if i < len(nxt):
                nxt[i]()
            if i < len(heads):
                heads[i]()
            if i == PROJ_CONV_SLOT and t > 0:
                conv_finish(t - 1)
    conv_finish(n_sub - 1)


def _proj_call(x2d, mod, pw, layer, rope_tabs, cache_bufs, *, latent, seq, tm):
    rows = x2d.shape[0]
    tiles_per_seq = seq // tm
    n_tiles = rows // tm
    if latent:
        mod_map = lambda i: (layer, 1 + i // tiles_per_seq, 0, 0)
    else:
        assert seq == PROJ_SUB, "the context path runs one sequence's attention per sub-tile"
        mod_map = lambda i: (layer, 0, 0, 0)
    row_spec = lambda w: pl.BlockSpec((tm, w), lambda i: (i, 0))
    in_specs = [row_spec(D_MODEL)]
    args = [x2d]
    if latent:
        blk = tm // SUBLANES
        last = rows // SUBLANES - 1
        in_specs += [
            pl.BlockSpec((SUBLANES, D_MODEL), lambda i: (jnp.maximum(i * blk - 1, 0), 0)),
            pl.BlockSpec((SUBLANES, D_MODEL), lambda i: (jnp.minimum((i + 1) * blk, last), 0)),
        ]
        args += [x2d, x2d]
    in_specs += [
        pl.BlockSpec((None, None, 1, 6 * D_MODEL), mod_map),
        _layer_spec((1, D_MODEL), layer),
        _layer_spec((D_MODEL, IN_COLS_PAD), layer),
        _layer_spec((1, Q_LORA), layer),
        _layer_spec((1, KV_LORA), layer),
        _layer_spec((Q_LORA, 2 * QK_W), layer),
        _layer_spec((KV_LORA, QK_W + MLA_W), layer),
        _layer_spec((SUBLANES, LANES), layer),
        _layer_spec((1, LANES), layer),
        _layer_spec((SUBLANES, CONV_W), layer),
    ]
    args += [mod, pw["g1"], pw["w_in"], pw["qng"], pw["kvng"], pw["wq"], pw["wkv"],
             pw["qhg"], pw["khg"], pw["convw"]]
    if latent:
        in_specs += [pl.BlockSpec((tm, LANES), lambda i: (i % tiles_per_seq, 0))] * 3
        args += list(rope_tabs)
        outs = [(QK_W, BF16), (QK_W, BF16), (MLA_W, BF16)]
        aliases = {}
    else:
        outs = [(MLA_W, BF16)]
    out_specs = [row_spec(w) for w, _ in outs]
    out_shape = [jax.ShapeDtypeStruct((rows, w), dt) for w, dt in outs]
    n_alias = 0
    if not latent:
        n_seq = tm // seq
        batch = rows // seq
        shapes = [(batch, DEPTH, seq, KV_LORA), (batch, DEPTH, QK_ROPE, seq)]
        if cache_bufs is None:
            aliases = {}
            out_specs += [pl.BlockSpec((n_seq,) + shp[1:], lambda i: (i, 0, 0, 0)) for shp in shapes]
        else:
            n_alias = len(cache_bufs)
            aliases = {len(args) + j: len(outs) + j for j in range(n_alias)}
            in_specs += [pl.BlockSpec(memory_space=pl.ANY)] * n_alias
            args += list(cache_bufs)
            out_specs += [pl.BlockSpec((n_seq, None) + shp[2:], lambda i: (i, layer, 0, 0))
                          for shp in shapes]
        out_shape += [jax.ShapeDtypeStruct(shp, F32) for shp in shapes]
    tail = [(CONV_W, BF16), (RET_W, BF16), (RET_W, BF16), (RET_W, BF16), (RET_W, F32)]
    out_specs += [row_spec(w) for w, _ in tail]
    out_shape += [jax.ShapeDtypeStruct((rows, w), dt) for w, dt in tail]
    return pl.pallas_call(
        functools.partial(_proj_kernel, latent=latent, tiles_per_seq=tiles_per_seq, tm=tm,
                          layer=layer, n_alias=n_alias),
        grid=(n_tiles,),
        in_specs=in_specs,
        out_specs=out_specs,
        out_shape=out_shape,
        input_output_aliases=aliases,
        compiler_params=pltpu.CompilerParams(
            dimension_semantics=("arbitrary",), vmem_limit_bytes=VMEM_LIMIT),
        name="proj_latent" if latent else "proj_ctx",
    )(*args)


def _attn_kernel(q_ref, k_ref, v_ref, kc_ref, vc_ref, o_ref, p_ref, pc_ref, *, unit):
    units = [(r0, j) for r0 in range(0, q_ref.shape[0], unit) for j in range(2)]
    n = len(units)

    def scores(u):
        r0, j = units[u]
        sl = slice(j * HEAD_PAD, (j + 1) * HEAD_PAD)
        q = q_ref[r0:r0 + unit, sl]
        return _dot_nt(q, k_ref[:, sl]), _dot_nt(q, kc_ref[:, sl])

    def softmax(u, s, sc):
        m = jnp.maximum(jnp.max(s, axis=-1, keepdims=True), jnp.max(sc, axis=-1, keepdims=True))
        p, pc = jnp.exp2(s - m), jnp.exp2(sc - m)
        p_ref[u % 2] = p.astype(BF16)
        pc_ref[u % 2] = pc.astype(BF16)
        return jnp.sum(p, axis=-1, keepdims=True) + jnp.sum(pc, axis=-1, keepdims=True)

    def values(u, l):
        o = _dot(p_ref[u % 2], v_ref[...]) + _dot(pc_ref[u % 2], vc_ref[...])
        return o * (1.0 / l)

    s_ahead = {0: scores(0), 1: scores(1)}
    l_ahead = {0: softmax(0, *s_ahead.pop(0))}
    pair = []
    for u, (r0, j) in enumerate(units):
        if u + 2 < n:
            s_ahead[u + 2] = scores(u + 2)
        if u + 1 < n:
            l_ahead[u + 1] = softmax(u + 1, *s_ahead.pop(u + 1))
        pair.append(values(u, l_ahead.pop(u)))
        if j == 1:
            o_ref[r0:r0 + unit, :] = _merge_head_pair(*pair)
            pair = []


def _attn_call(q, k, v, kc, vc, layer, *, tq, unit):
    b, t, _ = q.shape
    tk, tc = k.shape[1], kc.shape[2]
    return pl.pallas_call(
        functools.partial(_attn_kernel, unit=unit),
        grid=(b, MLA_HEADS // 2, t // tq),
        in_specs=[
            pl.BlockSpec((None, tq, 2 * HEAD_PAD), lambda bi, hp, i: (bi, i, hp)),
            pl.BlockSpec((None, tk, 2 * HEAD_PAD), lambda bi, hp, i: (bi, 0, hp)),
            pl.BlockSpec((None, tk, 2 * V_HEAD), lambda bi, hp, i: (bi, 0, hp)),
            pl.BlockSpec((None, None, tc, 2 * HEAD_PAD), lambda bi, hp, i: (layer, bi, 0, hp)),
            pl.BlockSpec((None, None, tc, 2 * V_HEAD), lambda bi, hp, i: (layer, bi, 0, hp)),
        ],
        out_specs=pl.BlockSpec((None, tq, 2 * V_HEAD), lambda bi, hp, i: (bi, i, hp)),
        out_shape=jax.ShapeDtypeStruct((b, t, MLA_W), BF16),
        scratch_shapes=[pltpu.VMEM((2, unit, tk), BF16), pltpu.VMEM((2, unit, tc), BF16)],
        compiler_params=pltpu.CompilerParams(
            dimension_semantics=("arbitrary", "arbitrary", "arbitrary"),
            vmem_limit_bytes=VMEM_LIMIT),
        name="attn_latent",
    )(q, k, v, kc, vc)


def _ret_kernel(*refs, latent, n_chunks, group, layer, n_alias):
    if latent:
        lgl_ref, lgc_ref, rq_ref, rk_ref, rv_ref, rg_ref, s0_ref, out_ref, o_acc, st_ref = refs
    else:
        lgl_ref, lgc_ref, rq_ref, rk_ref, rv_ref, rg_ref = refs[:6]
        out_ref, sfin_ref, o_acc, st_ref = refs[6 + n_alias:]
    c = RET_CHUNK
    seqs = range(group)
    lg_f, lg_b = lgl_ref[0:1, :], lgl_ref[1:2, :]
    pos = lax.broadcasted_iota(jnp.int32, (c, RET_W), 0).astype(F32)
    qdec_f = jnp.exp(lg_f * (pos + 1.0)).astype(BF16)
    kdec_f = jnp.exp(lg_f * (c - 1.0 - pos)).astype(BF16)
    qdec_b = jnp.exp(lg_b * (c - pos)).astype(BF16)
    kdec_b = jnp.exp(lg_b * pos).astype(BF16)
    cdec_f = jnp.exp(lg_f * float(c))
    cdec_b = jnp.exp(lg_b * float(c))
    ri = lax.broadcasted_iota(jnp.int32, (c, RET_HEADS * c), 0)
    ci = jnp.bitwise_and(lax.broadcasted_iota(jnp.int32, (c, RET_HEADS * c), 1), c - 1)
    diff = (ri - ci).astype(F32)
    dcat = (jnp.where(diff >= 0, jnp.exp(lgc_ref[0:1, :] * diff), 0.0)
            + jnp.where(diff <= 0, jnp.exp(lgc_ref[1:2, :] * (-diff)), 0.0))
    head_shift = RET_DV.bit_length() - 1
    lane_head = jnp.right_shift(lax.broadcasted_iota(jnp.int32, (c, RET_W), 1), head_shift)
    r_head = jnp.right_shift(lax.broadcasted_iota(jnp.int32, (RET_W, RET_W), 0), head_shift)
    c_head = jnp.right_shift(lax.broadcasted_iota(jnp.int32, (RET_W, RET_W), 1), head_shift)
    same_head = r_head == c_head
    head_mean = jnp.where(same_head, 1.0 / RET_DV, 0.0).astype(BF16)

    def stack_heads(a):
        zero = jnp.zeros_like(a)
        return jnp.concatenate([jnp.where(lane_head == hd, a, zero) for hd in range(RET_HEADS)], axis=0)

    def chunk(n):
        return n * c if isinstance(n, int) else pl.multiple_of(n * c, c)

    def load(sl):
        return ([rq_ref[g, sl, :] for g in seqs], [rk_ref[g, sl, :] for g in seqs],
                [rv_ref[g, sl, :] for g in seqs])

    def state_terms(d, qs, ks, vs, qdec, kdec):
        inc = [_dot((k * kdec).T, v) for k, v in zip(ks, vs)]
        zero = jnp.zeros((RET_W, RET_W), BF16)
        inter = [_dot(q * qdec, jnp.where(same_head, st_ref[g, d].astype(BF16), zero))
                 for g, q in zip(seqs, qs)]
        return inter, inc

    def state_update(d, inc, cdec):
        for g in seqs:
            st_ref[g, d] = st_ref[g, d] * cdec + inc[g]

    def finish(sl, os_):
        sq = [o * o for o in os_]
        hi = [x.astype(BF16) for x in sq]
        ms = [_dot(h, head_mean) + _dot((x - h.astype(F32)).astype(BF16), head_mean)
              for x, h in zip(sq, hi)]
        for g in seqs:
            y = os_[g] * lax.rsqrt(ms[g] + EPS)
            out_ref[g, sl, :] = (y * _silu(rg_ref[g, sl, :])).astype(BF16)

    def step(j, second_half):
        sl_f = pl.ds(chunk(j), c)
        sl_b = pl.ds(chunk(n_chunks - 1 - j), c)
        qf, kf, vf = load(sl_f)
        qb, kb, vb = load(sl_b)
        scores = [_dot_nt(q, stack_heads(k)) for q, k in zip(qf, kf)]
        inter_f, inc_f = state_terms(0, qf, kf, vf, qdec_f, kdec_f)
        inter_b, inc_b = state_terms(1, qb, kb, vb, qdec_b, kdec_b)
        o_f = [_dot((scores[g] * dcat).astype(BF16), stack_heads(vf[g])) + inter_f[g] for g in seqs]
        state_update(0, inc_f, cdec_f)
        state_update(1, inc_b, cdec_b)
        if second_half:
            finish(sl_f, [o_acc[g, sl_f, :] + o_f[g] for g in seqs])
            finish(sl_b, [o_acc[g, sl_b, :] + inter_b[g] for g in seqs])
        else:
            for g in seqs:
                o_acc[g, sl_f, :] = o_f[g]
                o_acc[g, sl_b, :] = inter_b[g]

    def scan(lo, hi, second_half):
        if hi - lo <= 1:
            for j in range(lo, hi):
                step(j, second_half)
        else:
            lax.fori_loop(lo, hi, lambda j, carry: (step(j, second_half), carry)[1], 0)

    st_ref[...] = jnp.zeros_like(st_ref)
    if latent:
        for g in seqs:
            for d in range(2):
                for hd in range(RET_HEADS):
                    st_ref[g, d, hd * RET_DK:(hd + 1) * RET_DK, hd * RET_DV:(hd + 1) * RET_DV] = s0_ref[g, d, hd]
    half = n_chunks // 2
    scan(0, half, False)
    scan(half, n_chunks, True)
    if not latent:
        for g in seqs:
            for d in range(2):
                for hd in range(RET_HEADS):
                    block = st_ref[g, d, hd * RET_DK:(hd + 1) * RET_DK, hd * RET_DV:(hd + 1) * RET_DV]
                    if n_alias:
                        sfin_ref[g, d, hd] = block
                    else:
                        for l in range(sfin_ref.shape[1]):
                            sfin_ref[g, l, d, hd] = block if l == layer else jnp.zeros_like(block)


def _ret_call(lgl, lgc, rq, rk, rv, rg, state_ret, layer, *, latent, group):
    b, t, _ = rq.shape
    assert (t // RET_CHUNK) % 2 == 0, "the two scans meet in the middle of an even chunk count"
    seq_spec = pl.BlockSpec((group, t, RET_W), lambda i: (i, 0, 0))
    st_block = (2, RET_HEADS, RET_DK, RET_DV)
    in_specs = [_layer_spec((SUBLANES, RET_W), layer),
                _layer_spec((SUBLANES, RET_HEADS * RET_CHUNK), layer),
                seq_spec, seq_spec, seq_spec, seq_spec]
    args = [lgl, lgc, rq, rk, rv, rg]
    out_specs = [seq_spec]
    out_shape = [jax.ShapeDtypeStruct((b, t, RET_W), BF16)]
    st_spec = pl.BlockSpec((group, None) + st_block, lambda i: (i, layer, 0, 0, 0, 0))
    aliases, n_alias = {}, 0
    if latent:
        in_specs.append(st_spec)
        args.append(state_ret)
    elif state_ret is None:
        out_specs.append(pl.BlockSpec((group, DEPTH) + st_block, lambda i: (i, 0, 0, 0, 0, 0)))
        out_shape.append(jax.ShapeDtypeStruct((b, DEPTH) + st_block, F32))
    else:
        in_specs.append(pl.BlockSpec(memory_space=pl.ANY))
        aliases, n_alias = {len(args): 1}, 1
        args.append(state_ret)
        out_specs.append(st_spec)
        out_shape.append(jax.ShapeDtypeStruct(state_ret.shape, state_ret.dtype))
    return pl.pallas_call(
        functools.partial(_ret_kernel, latent=latent, n_chunks=t // RET_CHUNK, group=group,
                          layer=layer, n_alias=n_alias),
        grid=(b // group,),
        in_specs=in_specs,
        out_specs=out_specs,
        out_shape=out_shape,
        input_output_aliases=aliases,
        scratch_shapes=[pltpu.VMEM((group, t, RET_W), F32),
                        pltpu.VMEM((group, 2, RET_W, RET_W), F32)],
        compiler_params=pltpu.CompilerParams(
            dimension_semantics=("arbitrary",), vmem_limit_bytes=VMEM_LIMIT),
        name="ret_latent" if latent else "ret_ctx",
    )(*args)


def _out_kernel(x_ref, attn_ref, conv_ref, ret_ref, mod_ref, g2_ref, wo_ref, wg_ref, wu_ref,
                wd_ref, o_ref, act_ref):
    subs = [slice(r0, r0 + OUT_SUB) for r0 in range(0, x_ref.shape[0], OUT_SUB)]
    x1 = []
    for rows in subs:
        mix = (_dot(attn_ref[rows, :], wo_ref[0:MLA_W, :])
               + _dot(conv_ref[rows, :], wo_ref[MLA_W:MLA_W + CONV_W, :])
               + _dot(ret_ref[rows, :], wo_ref[MLA_W + CONV_W:, :]))
        x1.append(x_ref[rows, :] + mod_ref[:, 2 * D_MODEL:3 * D_MODEL] * mix)
    for rows, xr in zip(subs, x1):
        h = ((_rms(xr) * g2_ref[...]) * (1.0 + mod_ref[:, 4 * D_MODEL:5 * D_MODEL])
             + mod_ref[:, 3 * D_MODEL:4 * D_MODEL]).astype(BF16)
        for j in range(FFN_HIDDEN // MXU_N):
            sl = slice(j * MXU_N, (j + 1) * MXU_N)
            gate = _dot(h, wg_ref[:, sl])
            up = _dot(h, wu_ref[:, sl])
            act_ref[rows, sl] = (_silu(gate) * up).astype(BF16)
        ffn = _dot(act_ref[rows, :], wd_ref[...])
        o_ref[rows, :] = xr + mod_ref[:, 5 * D_MODEL:6 * D_MODEL] * ffn


def _out_call(x2d, attn, conv, ret, mod, pw, layer, *, latent, seq, tm):
    rows = x2d.shape[0]
    tiles_per_seq = seq // tm
    if latent:
        mod_map = lambda i: (layer, 1 + i // tiles_per_seq, 0, 0)
    else:
        mod_map = lambda i: (layer, 0, 0, 0)
    row_spec = lambda w: pl.BlockSpec((tm, w), lambda i: (i, 0))
    return pl.pallas_call(
        _out_kernel,
        grid=(rows // tm,),
        in_specs=[
            row_spec(D_MODEL), row_spec(MLA_W), row_spec(CONV_W), row_spec(RET_W),
            pl.BlockSpec((None, None, 1, 6 * D_MODEL), mod_map),
            _layer_spec((1, D_MODEL), layer),
            _layer_spec((D_MODEL, D_MODEL), layer),
            _layer_spec((D_MODEL, FFN_HIDDEN), layer),
            _layer_spec((D_MODEL, FFN_HIDDEN), layer),
            _layer_spec((FFN_HIDDEN, D_MODEL), layer),
        ],
        out_specs=row_spec(D_MODEL),
        out_shape=jax.ShapeDtypeStruct((rows, D_MODEL), F32),
        scratch_shapes=[pltpu.VMEM((tm, FFN_HIDDEN), BF16)],
        compiler_params=pltpu.CompilerParams(
            dimension_semantics=("arbitrary",), vmem_limit_bytes=VMEM_LIMIT),
        name="out_latent" if latent else "out_ctx",
    )(x2d, attn, conv, ret, mod, pw["g2"], pw["wo"], pw["wg"], pw["wu"], pw["wd"])


def _pad_heads(w, heads, width):
    lead = w.shape[:-1]
    w = w.reshape(*lead, heads, width)
    w = jnp.pad(w, [(0, 0)] * len(lead) + [(0, 0), (0, HEAD_PAD - width)])
    return w.reshape(*lead, heads * HEAD_PAD)


def _swap_rope_pairs(w):
    nope = jnp.zeros_like(w[..., :QK_NOPE])
    rope = w[..., QK_NOPE:].reshape(*w.shape[:-1], 2, 2, ROPE_PAIR)
    rope = rope[..., ::-1, :].reshape(*w.shape[:-1], QK_ROPE)
    return jnp.concatenate([nope, rope], axis=-1)


def _w_in_layout_kernel(wt_ref, o_ref):
    k = wt_ref.shape[1]

    def put(dst, rows_t):
        o_ref[:, dst:dst + rows_t.shape[0]] = rows_t.T.astype(BF16)

    for c0 in range(0, C_KPE, LANES):
        put(c0, wt_ref[c0:c0 + LANES, :])
    put(C_KPE, jnp.concatenate([jnp.zeros((QK_NOPE, k), F32), wt_ref[C_KPE:C_KPE + QK_ROPE, :],
                                jnp.zeros((LANES - QK_HEAD, k), F32)], axis=0))
    shift = C_GB - (C_KPE + QK_ROPE)
    for c0 in range(C_GB, IN_COLS_PAD, MXU_N):
        put(c0, wt_ref[c0 - shift:c0 - shift + MXU_N, :])


def _w_in_layout_call(w_in):
    d, k, n = w_in.shape
    return pl.pallas_call(
        _w_in_layout_kernel,
        grid=(d,),
        in_specs=[pl.BlockSpec((None, n, k), lambda l: (l, 0, 0))],
        out_specs=pl.BlockSpec((None, k, IN_COLS_PAD), lambda l: (l, 0, 0)),
        out_shape=jax.ShapeDtypeStruct((d, k, IN_COLS_PAD), BF16),
        compiler_params=pltpu.CompilerParams(
            dimension_semantics=("arbitrary",), vmem_limit_bytes=VMEM_LIMIT),
        name="w_in_layout",
    )(jnp.swapaxes(w_in, 1, 2))


def _prep_weights(w_in, q_norm_g, kv_norm_g, w_q_up, w_kv_up, q_head_norm_g, k_head_norm_g,
                  conv_w, norm1_g, norm2_g, w_o, w_ffn_gate, w_ffn_up, w_ffn_down):
    d = DEPTH
    wq_h = w_q_up.reshape(d, Q_LORA, MLA_HEADS, QK_HEAD)
    wq = jnp.concatenate([_pad_heads(wq_h.reshape(d, Q_LORA, -1), MLA_HEADS, QK_HEAD),
                          _pad_heads(_swap_rope_pairs(wq_h).reshape(d, Q_LORA, -1), MLA_HEADS, QK_HEAD)],
                         axis=-1)
    kvu = w_kv_up.reshape(d, KV_LORA, MLA_HEADS, QK_NOPE + V_HEAD)
    wk = _pad_heads(kvu[..., :QK_NOPE].reshape(d, KV_LORA, -1), MLA_HEADS, QK_NOPE)
    wv = kvu[..., QK_NOPE:].reshape(d, KV_LORA, MLA_W)
    lane_pad = ((0, 0), (0, HEAD_PAD - QK_HEAD))
    qhg = jnp.stack([jnp.pad(q_head_norm_g, lane_pad),
                     jnp.pad(_swap_rope_pairs(q_head_norm_g), lane_pad)], axis=1)
    return {
        "g1": norm1_g.reshape(d, 1, D_MODEL),
        "g2": norm2_g.reshape(d, 1, D_MODEL),
        "w_in": _w_in_layout_call(w_in),
        "qng": q_norm_g.reshape(d, 1, Q_LORA),
        "kvng": kv_norm_g.reshape(d, 1, KV_LORA),
        "wq": wq.astype(BF16),
        "wkv": jnp.concatenate([wk, wv], axis=-1).astype(BF16),
        "qhg": jnp.pad(qhg, ((0, 0), (0, SUBLANES - 2), (0, 0))),
        "khg": jnp.pad(k_head_norm_g, lane_pad).reshape(d, 1, LANES),
        "convw": jnp.pad(conv_w, ((0, 0), (0, SUBLANES - 3), (0, 0))),
        "wo": w_o.astype(BF16),
        "wg": w_ffn_gate.astype(BF16),
        "wu": w_ffn_up.astype(BF16),
        "wd": w_ffn_down.astype(BF16),
    }


def _rope_tables(t):
    half = QK_ROPE // 2
    freqs = jnp.power(ROPE_THETA, -jnp.arange(0, half, 2, dtype=F32) / half)
    lane = jnp.arange(LANES)
    rel = lane - QK_NOPE
    in_rope = (rel >= 0) & (rel < QK_ROPE)
    freq_lane = freqs[jnp.clip(rel, 0, QK_ROPE - 1) % ROPE_PAIR]
    tok = jnp.arange(t)
    row = (tok // GRID_W).astype(F32)
    col = (tok % GRID_W).astype(F32)
    pos = jnp.where((rel < half)[None, :], row[:, None], col[:, None])
    ang = pos * freq_lane[None, :]
    first = in_rope & ((rel % half) < ROPE_PAIR)
    second = in_rope & ((rel % half) >= ROPE_PAIR)
    cos = jnp.where(in_rope[None, :], jnp.cos(ang), 1.0)
    sin = jnp.sin(ang)
    sin_lo = jnp.where(first[None, :], -sin, 0.0)
    sin_hi = jnp.where(second[None, :], sin, 0.0)
    return cos, sin_lo, sin_hi


def _decay_lanes(ret_decay_fwd, ret_decay_bwd):
    lg = jnp.stack([jax.nn.log_sigmoid(ret_decay_fwd.astype(F32)),
                    jax.nn.log_sigmoid(ret_decay_bwd.astype(F32))], axis=1)
    pad = ((0, 0), (0, SUBLANES - 2), (0, 0))
    return (jnp.pad(jnp.repeat(lg, RET_DV, axis=2), pad),
            jnp.pad(jnp.repeat(lg, RET_CHUNK, axis=2), pad))


TM = 1024
PROJ_SUB = 256
PROJ_CONV_SLOT = 1
TM_OUT = 1024
OUT_SUB = 256
RET_GROUP = 4
RET_GROUP_LATENT = 2
TQ = 2048
TQ_UNIT = 128
CTX_ATTN_LOOKAHEAD = 2


def kernel(x_prompt, x_sample, cache_ckv, cache_kpe, state_ret, c, c_ctx, ada_w, ada_b, norm1_g, norm2_g, w_in, q_norm_g, kv_norm_g, w_q_up, w_kv_up, q_head_norm_g, k_head_norm_g, conv_w, ret_decay_fwd, ret_decay_bwd, w_o, w_ffn_gate, w_ffn_up, w_ffn_down):
    bp, tp, _ = x_prompt.shape
    bs, ts, _ = x_sample.shape

    cond = jnp.concatenate([c_ctx[None, :], c, jnp.zeros((SUBLANES - 1 - bs, D_MODEL), F32)], axis=0)
    mods = _ada_call(cond, ada_w, ada_b).reshape(DEPTH, SUBLANES, 1, 6 * D_MODEL)

    pw = _prep_weights(w_in, q_norm_g, kv_norm_g, w_q_up, w_kv_up, q_head_norm_g, k_head_norm_g,
                       conv_w, norm1_g, norm2_g, w_o, w_ffn_gate, w_ffn_up, w_ffn_down)
    rope_tabs = _rope_tables(ts)
    lgl, lgc = _decay_lanes(ret_decay_fwd, ret_decay_bwd)
    kpe_ctx = jnp.pad(cache_kpe, ((0, 0), (0, 0), (0, 0), (QK_NOPE, LANES - QK_HEAD)))
    kc, vc = _ctx_kv_call(cache_ckv, kpe_ctx, pw["wkv"], pw["khg"])

    xp = x_prompt.reshape(bp * tp, D_MODEL)
    xs = x_sample.reshape(bs * ts, D_MODEL)
    caches, new_state = None, None
    for l in range(DEPTH):
        attn, *caches, conv, rq, rk, rv, rg = _proj_call(
            xp, mods, pw, l, None, caches, latent=False, seq=tp, tm=TM)
        r3 = lambda a: a.reshape(bp, tp, RET_W)
        ret, new_state = _ret_call(lgl, lgc, r3(rq), r3(rk), r3(rv), r3(rg), new_state, l,
                                   latent=False, group=RET_GROUP)
        xp = _out_call(xp, attn, conv, ret.reshape(bp * tp, RET_W), mods, pw, l,
                       latent=False, seq=tp, tm=TM_OUT)

        q, k, v, conv, rq, rk, rv, rg = _proj_call(
            xs, mods, pw, l, rope_tabs, (), latent=True, seq=ts, tm=TM)
        attn = _attn_call(q.reshape(bs, ts, QK_W), k.reshape(bs, ts, QK_W),
                          v.reshape(bs, ts, MLA_W), kc, vc, l, tq=TQ, unit=TQ_UNIT)
        r3 = lambda a: a.reshape(bs, ts, RET_W)
        (ret,) = _ret_call(lgl, lgc, r3(rq), r3(rk), r3(rv), r3(rg), state_ret, l,
                           latent=True, group=RET_GROUP_LATENT)
        xs = _out_call(xs, attn.reshape(bs * ts, MLA_W), conv, ret.reshape(bs * ts, RET_W),
                       mods, pw, l, latent=True, seq=ts, tm=TM_OUT)

    return (xp.reshape(bp, tp, D_MODEL), xs.reshape(bs, ts, D_MODEL),
            caches[0], jnp.swapaxes(caches[1], 2, 3), new_state)
```

```python
import functools

import jax
import jax.numpy as jnp
from jax import lax
from jax.experimental import pallas as pl
from jax.experimental.pallas import tpu as pltpu

D_MODEL = 1024
DEPTH = 2
GRID_W = 64
MLA_HEADS = 8
Q_LORA = 256
KV_LORA = 128
QK_NOPE = 64
QK_ROPE = 32
V_HEAD = 64
QK_HEAD = QK_NOPE + QK_ROPE
MLA_W = MLA_HEADS * V_HEAD
CONV_W = 256
RET_HEADS = 4
RET_DK = 64
RET_DV = 64
RET_W = RET_HEADS * RET_DV
RET_CHUNK = 128
FFN_HIDDEN = 2816
ROPE_THETA = 10000.0
EPS = 1e-6
LOG2_E = 1.4426950408889634

LANES = 128
SUBLANES = 8
MXU_N = 256
HEAD_PAD = LANES
QK_W = MLA_HEADS * HEAD_PAD
ROPE_PAIR = QK_ROPE // 4
VMEM_LIMIT = 56 * 1024 * 1024

C_QLAT = 0
C_KVLAT = C_QLAT + Q_LORA
C_KPE = C_KVLAT + KV_LORA
C_GB = C_KPE + LANES
C_GC = C_GB + CONV_W
C_XIN = C_GC + CONV_W
C_RQ = C_XIN + CONV_W
C_RK = C_RQ + RET_W
C_RV = C_RK + RET_W
C_RG = C_RV + RET_W
IN_COLS_PAD = C_RG + RET_W

BF16 = jnp.bfloat16
F32 = jnp.float32


def _dot(a, b):
    return jnp.dot(a, b, preferred_element_type=F32)


def _dot_nt(a, b):
    return lax.dot_general(a, b, (((1,), (1,)), ((), ())), preferred_element_type=F32)


def _rms(x, n=None):
    n = x.shape[-1] if n is None else n
    return x * lax.rsqrt(jnp.sum(x * x, axis=-1, keepdims=True) * (1.0 / n) + EPS)


def _silu(x):
    return x * (1.0 / (1.0 + jnp.exp(-x)))


def _const_spec(shape):
    nd = len(shape)
    return pl.BlockSpec(shape, lambda *_: (0,) * nd, pipeline_mode=pl.Buffered(1))


def _layer_spec(shape, layer):
    nd = len(shape)
    return pl.BlockSpec((None,) + tuple(shape), lambda *_: (layer,) + (0,) * nd,
                        pipeline_mode=pl.Buffered(1))


def _softmax_pv(s_list, v_list):
    m = functools.reduce(jnp.maximum, [jnp.max(s, axis=-1, keepdims=True) for s in s_list])
    l, o = None, None
    for s, v in zip(s_list, v_list):
        p = jnp.exp2(s - m)
        ls = jnp.sum(p, axis=-1, keepdims=True)
        os_ = _dot(p.astype(BF16), v)
        l = ls if l is None else l + ls
        o = os_ if o is None else o + os_
    return o * (1.0 / l)


def _merge_head_pair(o0, o1):
    lane = lax.broadcasted_iota(jnp.int32, o0.shape, 1)
    return jnp.where(lane < V_HEAD, o0, o1).astype(BF16)


ADA_TN = 1536


def _ada_kernel(cond_ref, w_ref, b_ref, o_ref):
    a = _silu(cond_ref[...]).astype(BF16)
    o_ref[...] = _dot(a, w_ref[...].astype(BF16)) + b_ref[...]


def _ada_call(cond, ada_w, ada_b):
    n = 6 * D_MODEL
    return pl.pallas_call(
        _ada_kernel,
        grid=(DEPTH, n // ADA_TN),
        in_specs=[
            pl.BlockSpec((SUBLANES, D_MODEL), lambda l, j: (0, 0)),
            pl.BlockSpec((None, D_MODEL, ADA_TN), lambda l, j: (l, 0, j)),
            pl.BlockSpec((None, 1, ADA_TN), lambda l, j: (l, 0, j)),
        ],
        out_specs=pl.BlockSpec((None, SUBLANES, ADA_TN), lambda l, j: (l, 0, j)),
        out_shape=jax.ShapeDtypeStruct((DEPTH, SUBLANES, n), F32),
        compiler_params=pltpu.CompilerParams(
            dimension_semantics=("arbitrary", "arbitrary"), vmem_limit_bytes=VMEM_LIMIT),
        name="ada_mod",
    )(cond, ada_w, ada_b.reshape(DEPTH, 1, n))


def _key_slab(kn, shared, pe_ssq, g):
    ssq = jnp.sum(kn * kn, axis=-1, keepdims=True) + pe_ssq
    return ((kn * g + shared) * lax.rsqrt(ssq * (1.0 / QK_HEAD) + EPS)).astype(BF16)


def _ctx_kv_kernel(ckv_ref, kpe_ref, wkv_ref, khg_ref, k_ref, v_ref):
    g = khg_ref[...]
    for i in range(ckv_ref.shape[0]):
        kv = _dot(ckv_ref[i].astype(BF16), wkv_ref[...])
        kpe = kpe_ref[i]
        shared = kpe * g
        pe_ssq = jnp.sum(kpe * kpe, axis=-1, keepdims=True)
        for hd in range(MLA_HEADS):
            sl = slice(hd * HEAD_PAD, (hd + 1) * HEAD_PAD)
            k_ref[i, :, sl] = _key_slab(kv[:, sl], shared, pe_ssq, g)
        v_ref[i] = kv[:, QK_W:].astype(BF16)


def _ctx_kv_call(cache_ckv, kpe_slab, wkv, khg):
    b, _, t, _ = cache_ckv.shape
    return pl.pallas_call(
        _ctx_kv_kernel,
        grid=(DEPTH,),
        in_specs=[
            pl.BlockSpec((b, None, t, KV_LORA), lambda l: (0, l, 0, 0)),
            pl.BlockSpec((b, None, t, LANES), lambda l: (0, l, 0, 0)),
            pl.BlockSpec((None, KV_LORA, QK_W + MLA_W), lambda l: (l, 0, 0)),
            pl.BlockSpec((None, 1, LANES), lambda l: (l, 0, 0)),
        ],
        out_specs=[
            pl.BlockSpec((None, b, t, QK_W), lambda l: (l, 0, 0, 0)),
            pl.BlockSpec((None, b, t, MLA_W), lambda l: (l, 0, 0, 0)),
        ],
        out_shape=[
            jax.ShapeDtypeStruct((DEPTH, b, t, QK_W), BF16),
            jax.ShapeDtypeStruct((DEPTH, b, t, MLA_W), BF16),
        ],
        compiler_params=pltpu.CompilerParams(
            dimension_semantics=("arbitrary",), vmem_limit_bytes=VMEM_LIMIT),
        name="ctx_kv",
    )(cache_ckv, kpe_slab, wkv, khg)


def _store_layer_slot(ref, idx, layer, n_alias, value):
    if n_alias:
        ref[idx] = value
    else:
        for l in range(ref.shape[1]):
            ref[idx, l] = value if l == layer else jnp.zeros_like(value)


def _proj_kernel(*refs, latent, tiles_per_seq, tm, layer, n_alias):
    it = iter(refs)
    x_ref = next(it)
    if latent:
        xp_ref, xn_ref = next(it), next(it)
    mod_ref, g1_ref, win_ref, qng_ref, kvng_ref = (next(it) for _ in range(5))
    wq_ref, wkv_ref, qhg_ref, khg_ref, cw_ref = (next(it) for _ in range(5))
    if latent:
        rope_row_ref, rope_col_ref = next(it), next(it)
        q_out, k_out, v_out = next(it), next(it), next(it)
    else:
        for _ in range(n_alias):
            next(it)
        attn_out, ckv_out, kpe_out = next(it), next(it), next(it)
    conv_out, rq_out, rk_out, rv_out, rg_out = (next(it) for _ in range(5))

    shift = mod_ref[:, 0:D_MODEL]
    scale1 = 1.0 + mod_ref[:, D_MODEL:2 * D_MODEL]
    g1 = g1_ref[...]

    def norm_mod(x):
        return ((_rms(x) * g1) * scale1 + shift).astype(BF16)

    q_scale = QK_HEAD ** -0.5 * LOG2_E
    qg = qhg_ref[0:1, :] * q_scale
    kg = khg_ref[...]
    head_slice = lambda hd: slice(hd * HEAD_PAD, (hd + 1) * HEAD_PAD)
    n_sub = tm // PROJ_SUB
    subs = [slice(t * PROJ_SUB, (t + 1) * PROJ_SUB) for t in range(n_sub)]
    state = [dict() for _ in subs]

    def rope_tables(t):
        lane = lax.broadcasted_iota(jnp.int32, (GRID_W, LANES), 1)
        row_lane = (lane >= QK_NOPE) & (lane < QK_NOPE + QK_ROPE // 2)
        tile_in_seq = pl.program_id(0) % tiles_per_seq
        grid_row0 = tile_in_seq * (tm // GRID_W) + t * (PROJ_SUB // GRID_W)
        tabs = []
        for k in range(3):
            blocks = [jnp.where(row_lane, rope_row_ref[k, pl.ds(grid_row0 + b, 1), :], rope_col_ref[k])
                      for b in range(PROJ_SUB // GRID_W)]
            tabs.append(jnp.concatenate(blocks, axis=0))
        return tabs

    def projection_stage(t):
        rows, st = subs[t], state[t]

        def proj(col, width):
            return _dot(st["h"], win_ref[:, col:col + width])

        def latents():
            st["h"] = norm_mod(x_ref[rows, :])
            st["q_lat"] = proj(C_QLAT, Q_LORA)
            st["kv_pe"] = proj(C_KVLAT, KV_LORA + LANES)

        def conv_inputs():
            st["u"] = proj(C_GC, CONV_W) * proj(C_XIN, CONV_W)
            if latent and t == 0:
                hh = norm_mod(jnp.concatenate([xp_ref[...], xn_ref[...]], axis=0))
                halo = (_dot(hh, win_ref[:, C_GC:C_GC + CONV_W])
                        * _dot(hh, win_ref[:, C_XIN:C_XIN + CONV_W]))
                i = pl.program_id(0)
                has_prev = jnp.where(i % tiles_per_seq != 0, 1.0, 0.0)
                has_next = jnp.where(i % tiles_per_seq != tiles_per_seq - 1, 1.0, 0.0)
                state[0]["u_before"] = halo[SUBLANES - 1:SUBLANES, :] * has_prev
                state[-1]["u_after"] = halo[SUBLANES:SUBLANES + 1, :] * has_next

        def queries():
            qn = (_rms(st["q_lat"]) * qng_ref[...]).astype(BF16)
            st["q"] = _dot(qn, wq_ref[:, 0:QK_W])
            if latent:
                st["q_sw"] = _dot(qn, wq_ref[:, QK_W:2 * QK_W])
                cos, sin_lo, sin_hi = st["rope"] = rope_tables(t)
                st["q_tab"] = qg * cos
                st["q_tab_sw"] = (qhg_ref[1:2, :] * q_scale) * (sin_lo + sin_hi)

        def keys_values():
            ckv = _rms(st["kv_pe"][:, 0:KV_LORA]) * kvng_ref[...]
            kpe = st["kv_pe"][:, KV_LORA:]
            st["kv"] = _dot(ckv.astype(BF16), wkv_ref[...])
            shared = kpe * kg
            if latent:
                cos, sin_lo, sin_hi = st["rope"]
                shared = (shared * cos + pltpu.roll(shared, LANES - ROPE_PAIR, axis=1) * sin_lo
                          + pltpu.roll(shared, ROPE_PAIR, axis=1) * sin_hi)
            else:
                _store_layer_slot(ckv_out, t, layer, n_alias, ckv)
                _store_layer_slot(kpe_out, t, layer, n_alias, kpe.T[QK_NOPE:QK_HEAD, :])
            st["shared"] = shared
            st["pe_ssq"] = jnp.sum(kpe * kpe, axis=-1, keepdims=True)

        def conv_gate():
            st["gb"] = proj(C_GB, CONV_W)

        def ret_q():
            rq_out[rows, :] = proj(C_RQ, RET_W).astype(BF16)

        def ret_k():
            rk_out[rows, :] = (proj(C_RK, RET_W) * (RET_DK ** -0.5)).astype(BF16)

        def ret_v():
            rv_out[rows, :] = proj(C_RV, RET_W).astype(BF16)

        def ret_g():
            rg_out[rows, :] = proj(C_RG, RET_W)

        return [latents, conv_inputs, queries, keys_values, conv_gate, ret_q, ret_k, ret_v, ret_g]

    def conv_finish(t):
        st, u = state[t], state[t]["u"]
        zero = jnp.zeros((1, CONV_W), F32)
        if latent:
            u_before = st["u_before"] if t == 0 else state[t - 1]["u"][PROJ_SUB - 1:PROJ_SUB, :]
            u_after = st["u_after"] if t == n_sub - 1 else state[t + 1]["u"][0:1, :]
        else:
            u_before, u_after = zero, zero
        row = lax.broadcasted_iota(jnp.int32, (PROJ_SUB, CONV_W), 0)
        u_m1 = jnp.where(row == 0, u_before, pltpu.roll(u, 1, axis=0))
        u_p1 = jnp.where(row == PROJ_SUB - 1, u_after, pltpu.roll(u, PROJ_SUB - 1, axis=0))
        y = u_m1 * cw_ref[0:1, :] + u * cw_ref[1:2, :] + u_p1 * cw_ref[2:3, :]
        conv_out[subs[t], :] = (st["gb"] * y).astype(BF16)

    def head_stage(t):
        rows, st = subs[t], state[t]

        def query_slab(sl):
            qs = st["q"][:, sl]
            r = lax.rsqrt(jnp.sum(qs * qs, axis=-1, keepdims=True) * (1.0 / QK_HEAD) + EPS)
            if latent:
                return ((qs * st["q_tab"] + st["q_sw"][:, sl] * st["q_tab_sw"]) * r).astype(BF16)
            return (qs * qg * r).astype(BF16)

        def key_slab(sl):
            return _key_slab(st["kv"][:, sl], st["shared"], st["pe_ssq"], kg)

        if latent:
            def store_head(hd):
                sl = head_slice(hd)
                q_out[rows, sl] = query_slab(sl)
                k_out[rows, sl] = key_slab(sl)
                if hd == MLA_HEADS - 1:
                    v_out[rows, :] = st["kv"][:, QK_W:].astype(BF16)
            return [functools.partial(store_head, hd) for hd in range(MLA_HEADS)]

        ahead, pair = [], []

        def scores(hd):
            sl = head_slice(hd)
            return _dot_nt(query_slab(sl), key_slab(sl))

        def attend(hd):
            if hd == 0:
                ahead.extend(scores(j) for j in range(CTX_ATTN_LOOKAHEAD))
            s = ahead.pop(0)
            if hd + CTX_ATTN_LOOKAHEAD < MLA_HEADS:
                ahead.append(scores(hd + CTX_ATTN_LOOKAHEAD))
            hp = hd // 2
            v_pair = st["kv"][:, QK_W + hp * LANES:QK_W + (hp + 1) * LANES].astype(BF16)
            pair.append(_softmax_pv([s], [v_pair]))
            if hd % 2 == 1:
                attn_out[rows, hp * LANES:(hp + 1) * LANES] = _merge_head_pair(*pair)
                pair.clear()

        return [functools.partial(attend, hd) for hd in range(MLA_HEADS)]

    for piece in projection_stage(0):
        piece()
    for t in range(n_sub):
        heads = head_stage(t)
        nxt = projection_stage(t + 1) if t + 1 < n_sub else []
        for i in range(max(len(heads), len(nxt))):
            if i < len(nxt):
                nxt[i]()
            if i < len(heads):
                heads[i]()
            if i == PROJ_CONV_SLOT and t > 0:
                conv_finish(t - 1)
    conv_finish(n_sub - 1)


def _proj_call(x2d, mod, pw, layer, rope_tabs, cache_bufs, *, latent, seq, tm):
    rows = x2d.shape[0]
    tiles_per_seq = seq // tm
    n_tiles = rows // tm
    if latent:
        mod_map = lambda i: (layer, 1 + i // tiles_per_seq, 0, 0)
    else:
        assert seq == PROJ_SUB, "the context path runs one sequence's attention per sub-tile"
        mod_map = lambda i: (layer, 0, 0, 0)
    row_spec = lambda w: pl.BlockSpec((tm, w), lambda i: (i, 0))
    in_specs = [row_spec(D_MODEL)]
    args = [x2d]
    if latent:
        blk = tm // SUBLANES
        last = rows // SUBLANES - 1
        in_specs += [
            pl.BlockSpec((SUBLANES, D_MODEL), lambda i: (jnp.maximum(i * blk - 1, 0), 0)),
            pl.BlockSpec((SUBLANES, D_MODEL), lambda i: (jnp.minimum((i + 1) * blk, last), 0)),
        ]
        args += [x2d, x2d]
    in_specs += [
        pl.BlockSpec((None, None, 1, 6 * D_MODEL), mod_map),
        _layer_spec((1, D_MODEL), layer),
        _layer_spec((D_MODEL, IN_COLS_PAD), layer),
        _layer_spec((1, Q_LORA), layer),
        _layer_spec((1, KV_LORA), layer),
        _layer_spec((Q_LORA, 2 * QK_W), layer),
        _layer_spec((KV_LORA, QK_W + MLA_W), layer),
        _layer_spec((SUBLANES, LANES), layer),
        _layer_spec((1, LANES), layer),
        _layer_spec((SUBLANES, CONV_W), layer),
    ]
    args += [mod, pw["g1"], pw["w_in"], pw["qng"], pw["kvng"], pw["wq"], pw["wkv"],
             pw["qhg"], pw["khg"], pw["convw"]]
    if latent:
        in_specs += [_const_spec(tab.shape) for tab in rope_tabs]
        args += list(rope_tabs)
        outs = [(QK_W, BF16), (QK_W, BF16), (MLA_W, BF16)]
        aliases = {}
    else:
        outs = [(MLA_W, BF16)]
    out_specs = [row_spec(w) for w, _ in outs]
    out_shape = [jax.ShapeDtypeStruct((rows, w), dt) for w, dt in outs]
    n_alias = 0
    if not latent:
        n_seq = tm // seq
        batch = rows // seq
        shapes = [(batch, DEPTH, seq, KV_LORA), (batch, DEPTH, QK_ROPE, seq)]
        if cache_bufs is None:
            aliases = {}
            out_specs += [pl.BlockSpec((n_seq,) + shp[1:], lambda i: (i, 0, 0, 0)) for shp in shapes]
        else:
            n_alias = len(cache_bufs)
            aliases = {len(args) + j: len(outs) + j for j in range(n_alias)}
            in_specs += [pl.BlockSpec(memory_space=pl.ANY)] * n_alias
            args += list(cache_bufs)
            out_specs += [pl.BlockSpec((n_seq, None) + shp[2:], lambda i: (i, layer, 0, 0))
                          for shp in shapes]
        out_shape += [jax.ShapeDtypeStruct(shp, F32) for shp in shapes]
    tail = [(CONV_W, BF16), (RET_W, BF16), (RET_W, BF16), (RET_W, BF16), (RET_W, F32)]
    out_specs += [row_spec(w) for w, _ in tail]
    out_shape += [jax.ShapeDtypeStruct((rows, w), dt) for w, dt in tail]
    return pl.pallas_call(
        functools.partial(_proj_kernel, latent=latent, tiles_per_seq=tiles_per_seq, tm=tm,
                          layer=layer, n_alias=n_alias),
        grid=(n_tiles,),
        in_specs=in_specs,
        out_specs=out_specs,
        out_shape=out_shape,
        input_output_aliases=aliases,
        compiler_params=pltpu.CompilerParams(
            dimension_semantics=("arbitrary",), vmem_limit_bytes=VMEM_LIMIT),
        name="proj_latent" if latent else "proj_ctx",
    )(*args)


def _attn_kernel(q_ref, k_ref, v_ref, kc_ref, vc_ref, o_ref, p_ref, pc_ref, *, unit):
    units = [(r0, j) for r0 in range(0, q_ref.shape[0], unit) for j in range(2)]
    n = len(units)

    def scores(u):
        r0, j = units[u]
        sl = slice(j * HEAD_PAD, (j + 1) * HEAD_PAD)
        q = q_ref[r0:r0 + unit, sl]
        return _dot_nt(q, k_ref[:, sl]), _dot_nt(q, kc_ref[:, sl])

    def softmax(u, s, sc):
        m = jnp.maximum(jnp.max(s, axis=-1, keepdims=True), jnp.max(sc, axis=-1, keepdims=True))
        p, pc = jnp.exp2(s - m), jnp.exp2(sc - m)
        p_ref[u % 2] = p.astype(BF16)
        pc_ref[u % 2] = pc.astype(BF16)
        return jnp.sum(p, axis=-1, keepdims=True) + jnp.sum(pc, axis=-1, keepdims=True)

    def values(u, l):
        o = _dot(p_ref[u % 2], v_ref[...]) + _dot(pc_ref[u % 2], vc_ref[...])
        return o * (1.0 / l)

    s_ahead = {0: scores(0), 1: scores(1)}
    l_ahead = {0: softmax(0, *s_ahead.pop(0))}
    pair = []
    for u, (r0, j) in enumerate(units):
        if u + 2 < n:
            s_ahead[u + 2] = scores(u + 2)
        if u + 1 < n:
            l_ahead[u + 1] = softmax(u + 1, *s_ahead.pop(u + 1))
        pair.append(values(u, l_ahead.pop(u)))
        if j == 1:
            o_ref[r0:r0 + unit, :] = _merge_head_pair(*pair)
            pair = []


def _attn_call(q, k, v, kc, vc, layer, *, tq, unit):
    b, t, _ = q.shape
    tk, tc = k.shape[1], kc.shape[2]
    return pl.pallas_call(
        functools.partial(_attn_kernel, unit=unit),
        grid=(b, MLA_HEADS // 2, t // tq),
        in_specs=[
            pl.BlockSpec((None, tq, 2 * HEAD_PAD), lambda bi, hp, i: (bi, i, hp)),
            pl.BlockSpec((None, tk, 2 * HEAD_PAD), lambda bi, hp, i: (bi, 0, hp)),
            pl.BlockSpec((None, tk, 2 * V_HEAD), lambda bi, hp, i: (bi, 0, hp)),
            pl.BlockSpec((None, None, tc, 2 * HEAD_PAD), lambda bi, hp, i: (layer, bi, 0, hp)),
            pl.BlockSpec((None, None, tc, 2 * V_HEAD), lambda bi, hp, i: (layer, bi, 0, hp)),
        ],
        out_specs=pl.BlockSpec((None, tq, 2 * V_HEAD), lambda bi, hp, i: (bi, i, hp)),
        out_shape=jax.ShapeDtypeStruct((b, t, MLA_W), BF16),
        scratch_shapes=[pltpu.VMEM((2, unit, tk), BF16), pltpu.VMEM((2, unit, tc), BF16)],
        compiler_params=pltpu.CompilerParams(
            dimension_semantics=("arbitrary", "arbitrary", "arbitrary"),
            vmem_limit_bytes=VMEM_LIMIT),
        name="attn_latent",
    )(q, k, v, kc, vc)


def _ret_kernel(*refs, latent, n_chunks, group, layer, n_alias):
    if latent:
        lgl_ref, lgc_ref, rq_ref, rk_ref, rv_ref, rg_ref, s0_ref, out_ref, o_acc, st_ref = refs
    else:
        lgl_ref, lgc_ref, rq_ref, rk_ref, rv_ref, rg_ref = refs[:6]
        out_ref, sfin_ref, o_acc, st_ref = refs[6 + n_alias:]
    c = RET_CHUNK
    seqs = range(group)
    lg_f, lg_b = lgl_ref[0:1, :], lgl_ref[1:2, :]
    pos = lax.broadcasted_iota(jnp.int32, (c, RET_W), 0).astype(F32)
    qdec_f = jnp.exp(lg_f * (pos + 1.0)).astype(BF16)
    kdec_f = jnp.exp(lg_f * (c - 1.0 - pos)).astype(BF16)
    qdec_b = jnp.exp(lg_b * (c - pos)).astype(BF16)
    kdec_b = jnp.exp(lg_b * pos).astype(BF16)
    cdec_f = jnp.exp(lg_f * float(c))
    cdec_b = jnp.exp(lg_b * float(c))
    ri = lax.broadcasted_iota(jnp.int32, (c, RET_HEADS * c), 0)
    ci = jnp.bitwise_and(lax.broadcasted_iota(jnp.int32, (c, RET_HEADS * c), 1), c - 1)
    diff = (ri - ci).astype(F32)
    dcat = (jnp.where(diff >= 0, jnp.exp(lgc_ref[0:1, :] * diff), 0.0)
            + jnp.where(diff <= 0, jnp.exp(lgc_ref[1:2, :] * (-diff)), 0.0))
    head_shift = RET_DV.bit_length() - 1
    lane_head = jnp.right_shift(lax.broadcasted_iota(jnp.int32, (c, RET_W), 1), head_shift)
    r_head = jnp.right_shift(lax.broadcasted_iota(jnp.int32, (RET_W, RET_W), 0), head_shift)
    c_head = jnp.right_shift(lax.broadcasted_iota(jnp.int32, (RET_W, RET_W), 1), head_shift)
    same_head = r_head == c_head
    head_mean = jnp.where(same_head, 1.0 / RET_DV, 0.0).astype(BF16)

    def stack_heads(a):
        zero = jnp.zeros_like(a)
        return jnp.concatenate([jnp.where(lane_head == hd, a, zero) for hd in range(RET_HEADS)], axis=0)

    def chunk(n):
        return n * c if isinstance(n, int) else pl.multiple_of(n * c, c)

    def load(sl):
        return ([rq_ref[g, sl, :] for g in seqs], [rk_ref[g, sl, :] for g in seqs],
                [rv_ref[g, sl, :] for g in seqs])

    def state_terms(d, qs, ks, vs, qdec, kdec):
        inc = [_dot((k * kdec).T, v) for k, v in zip(ks, vs)]
        zero = jnp.zeros((RET_W, RET_W), BF16)
        inter = [_dot(q * qdec, jnp.where(same_head, st_ref[g, d].astype(BF16), zero))
                 for g, q in zip(seqs, qs)]
        return inter, inc

    def state_update(d, inc, cdec):
        for g in seqs:
            st_ref[g, d] = st_ref[g, d] * cdec + inc[g]

    def finish(sl, os_):
        sq = [o * o for o in os_]
        hi = [x.astype(BF16) for x in sq]
        ms = [_dot(h, head_mean) + _dot((x - h.astype(F32)).astype(BF16), head_mean)
              for x, h in zip(sq, hi)]
        for g in seqs:
            y = os_[g] * lax.rsqrt(ms[g] + EPS)
            out_ref[g, sl, :] = (y * _silu(rg_ref[g, sl, :])).astype(BF16)

    def step(j, second_half):
        sl_f = pl.ds(chunk(j), c)
        sl_b = pl.ds(chunk(n_chunks - 1 - j), c)
        qf, kf, vf = load(sl_f)
        qb, kb, vb = load(sl_b)
        scores = [_dot_nt(q, stack_heads(k)) for q, k in zip(qf, kf)]
        inter_f, inc_f = state_terms(0, qf, kf, vf, qdec_f, kdec_f)
        inter_b, inc_b = state_terms(1, qb, kb, vb, qdec_b, kdec_b)
        o_f = [_dot((scores[g] * dcat).astype(BF16), stack_heads(vf[g])) + inter_f[g] for g in seqs]
        state_update(0, inc_f, cdec_f)
        state_update(1, inc_b, cdec_b)
        if second_half:
            finish(sl_f, [o_acc[g, sl_f, :] + o_f[g] for g in seqs])
            finish(sl_b, [o_acc[g, sl_b, :] + inter_b[g] for g in seqs])
        else:
            for g in seqs:
                o_acc[g, sl_f, :] = o_f[g]
                o_acc[g, sl_b, :] = inter_b[g]

    def scan(lo, hi, second_half):
        if hi - lo <= 1:
            for j in range(lo, hi):
                step(j, second_half)
        else:
            lax.fori_loop(lo, hi, lambda j, carry: (step(j, second_half), carry)[1], 0)

    st_ref[...] = jnp.zeros_like(st_ref)
    if latent:
        for g in seqs:
            for d in range(2):
                for hd in range(RET_HEADS):
                    st_ref[g, d, hd * RET_DK:(hd + 1) * RET_DK, hd * RET_DV:(hd + 1) * RET_DV] = s0_ref[g, d, hd]
    half = n_chunks // 2
    scan(0, half, False)
    scan(half, n_chunks, True)
    if not latent:
        for g in seqs:
            for d in range(2):
                for hd in range(RET_HEADS):
                    block = st_ref[g, d, hd * RET_DK:(hd + 1) * RET_DK, hd * RET_DV:(hd + 1) * RET_DV]
                    if n_alias:
                        sfin_ref[g, d, hd] = block
                    else:
                        for l in range(sfin_ref.shape[1]):
                            sfin_ref[g, l, d, hd] = block if l == layer else jnp.zeros_like(block)


def _ret_call(lgl, lgc, rq, rk, rv, rg, state_ret, layer, *, latent, group):
    b, t, _ = rq.shape
    assert (t // RET_CHUNK) % 2 == 0, "the two scans meet in the middle of an even chunk count"
    seq_spec = pl.BlockSpec((group, t, RET_W), lambda i: (i, 0, 0))
    st_block = (2, RET_HEADS, RET_DK, RET_DV)
    in_specs = [_layer_spec((SUBLANES, RET_W), layer),
                _layer_spec((SUBLANES, RET_HEADS * RET_CHUNK), layer),
                seq_spec, seq_spec, seq_spec, seq_spec]
    args = [lgl, lgc, rq, rk, rv, rg]
    out_specs = [seq_spec]
    out_shape = [jax.ShapeDtypeStruct((b, t, RET_W), BF16)]
    st_spec = pl.BlockSpec((group, None) + st_block, lambda i: (i, layer, 0, 0, 0, 0))
    aliases, n_alias = {}, 0
    if latent:
        in_specs.append(st_spec)
        args.append(state_ret)
    elif state_ret is None:
        out_specs.append(pl.BlockSpec((group, DEPTH) + st_block, lambda i: (i, 0, 0, 0, 0, 0)))
        out_shape.append(jax.ShapeDtypeStruct((b, DEPTH) + st_block, F32))
    else:
        in_specs.append(pl.BlockSpec(memory_space=pl.ANY))
        aliases, n_alias = {len(args): 1}, 1
        args.append(state_ret)
        out_specs.append(st_spec)
        out_shape.append(jax.ShapeDtypeStruct(state_ret.shape, state_ret.dtype))
    return pl.pallas_call(
        functools.partial(_ret_kernel, latent=latent, n_chunks=t // RET_CHUNK, group=group,
                          layer=layer, n_alias=n_alias),
        grid=(b // group,),
        in_specs=in_specs,
        out_specs=out_specs,
        out_shape=out_shape,
        input_output_aliases=aliases,
        scratch_shapes=[pltpu.VMEM((group, t, RET_W), F32),
                        pltpu.VMEM((group, 2, RET_W, RET_W), F32)],
        compiler_params=pltpu.CompilerParams(
            dimension_semantics=("arbitrary",), vmem_limit_bytes=VMEM_LIMIT),
        name="ret_latent" if latent else "ret_ctx",
    )(*args)


def _out_kernel(x_ref, attn_ref, conv_ref, ret_ref, mod_ref, g2_ref, wo_ref, wg_ref, wu_ref,
                wd_ref, o_ref, act_ref):
    subs = [slice(r0, r0 + OUT_SUB) for r0 in range(0, x_ref.shape[0], OUT_SUB)]
    x1 = []
    for rows in subs:
        mix = (_dot(attn_ref[rows, :], wo_ref[0:MLA_W, :])
               + _dot(conv_ref[rows, :], wo_ref[MLA_W:MLA_W + CONV_W, :])
               + _dot(ret_ref[rows, :], wo_ref[MLA_W + CONV_W:, :]))
        x1.append(x_ref[rows, :] + mod_ref[:, 2 * D_MODEL:3 * D_MODEL] * mix)
    for rows, xr in zip(subs, x1):
        h = ((_rms(xr) * g2_ref[...]) * (1.0 + mod_ref[:, 4 * D_MODEL:5 * D_MODEL])
             + mod_ref[:, 3 * D_MODEL:4 * D_MODEL]).astype(BF16)
        for j in range(FFN_HIDDEN // MXU_N):
            sl = slice(j * MXU_N, (j + 1) * MXU_N)
            gate = _dot(h, wg_ref[:, sl])
            up = _dot(h, wu_ref[:, sl])
            act_ref[rows, sl] = (_silu(gate) * up).astype(BF16)
        ffn = _dot(act_ref[rows, :], wd_ref[...])
        o_ref[rows, :] = xr + mod_ref[:, 5 * D_MODEL:6 * D_MODEL] * ffn


def _out_call(x2d, attn, conv, ret, mod, pw, layer, *, latent, seq, tm):
    rows = x2d.shape[0]
    tiles_per_seq = seq // tm
    if latent:
        mod_map = lambda i: (layer, 1 + i // tiles_per_seq, 0, 0)
    else:
        mod_map = lambda i: (layer, 0, 0, 0)
    row_spec = lambda w: pl.BlockSpec((tm, w), lambda i: (i, 0))
    return pl.pallas_call(
        _out_kernel,
        grid=(rows // tm,),
        in_specs=[
            row_spec(D_MODEL), row_spec(MLA_W), row_spec(CONV_W), row_spec(RET_W),
            pl.BlockSpec((None, None, 1, 6 * D_MODEL), mod_map),
            _layer_spec((1, D_MODEL), layer),
            _layer_spec((D_MODEL, D_MODEL), layer),
            _layer_spec((D_MODEL, FFN_HIDDEN), layer),
            _layer_spec((D_MODEL, FFN_HIDDEN), layer),
            _layer_spec((FFN_HIDDEN, D_MODEL), layer),
        ],
        out_specs=row_spec(D_MODEL),
        out_shape=jax.ShapeDtypeStruct((rows, D_MODEL), F32),
        scratch_shapes=[pltpu.VMEM((tm, FFN_HIDDEN), BF16)],
        compiler_params=pltpu.CompilerParams(
            dimension_semantics=("arbitrary",), vmem_limit_bytes=VMEM_LIMIT),
        name="out_latent" if latent else "out_ctx",
    )(x2d, attn, conv, ret, mod, pw["g2"], pw["wo"], pw["wg"], pw["wu"], pw["wd"])


def _pad_heads(w, heads, width):
    lead = w.shape[:-1]
    w = w.reshape(*lead, heads, width)
    w = jnp.pad(w, [(0, 0)] * len(lead) + [(0, 0), (0, HEAD_PAD - width)])
    return w.reshape(*lead, heads * HEAD_PAD)


def _swap_rope_pairs(w):
    nope = jnp.zeros_like(w[..., :QK_NOPE])
    rope = w[..., QK_NOPE:].reshape(*w.shape[:-1], 2, 2, ROPE_PAIR)
    rope = rope[..., ::-1, :].reshape(*w.shape[:-1], QK_ROPE)
    return jnp.concatenate([nope, rope], axis=-1)


def _w_in_layout_kernel(wt_ref, o_ref):
    k = wt_ref.shape[1]

    def put(dst, rows_t):
        o_ref[:, dst:dst + rows_t.shape[0]] = rows_t.T.astype(BF16)

    for c0 in range(0, C_KPE, LANES):
        put(c0, wt_ref[c0:c0 + LANES, :])
    put(C_KPE, jnp.concatenate([jnp.zeros((QK_NOPE, k), F32), wt_ref[C_KPE:C_KPE + QK_ROPE, :],
                                jnp.zeros((LANES - QK_HEAD, k), F32)], axis=0))
    shift = C_GB - (C_KPE + QK_ROPE)
    for c0 in range(C_GB, IN_COLS_PAD, MXU_N):
        put(c0, wt_ref[c0 - shift:c0 - shift + MXU_N, :])


def _w_in_layout_call(w_in):
    d, k, n = w_in.shape
    return pl.pallas_call(
        _w_in_layout_kernel,
        grid=(d,),
        in_specs=[pl.BlockSpec((None, n, k), lambda l: (l, 0, 0))],
        out_specs=pl.BlockSpec((None, k, IN_COLS_PAD), lambda l: (l, 0, 0)),
        out_shape=jax.ShapeDtypeStruct((d, k, IN_COLS_PAD), BF16),
        compiler_params=pltpu.CompilerParams(
            dimension_semantics=("arbitrary",), vmem_limit_bytes=VMEM_LIMIT),
        name="w_in_layout",
    )(jnp.swapaxes(w_in, 1, 2))


def _prep_weights(w_in, q_norm_g, kv_norm_g, w_q_up, w_kv_up, q_head_norm_g, k_head_norm_g,
                  conv_w, norm1_g, norm2_g, w_o, w_ffn_gate, w_ffn_up, w_ffn_down):
    d = DEPTH
    wq_h = w_q_up.reshape(d, Q_LORA, MLA_HEADS, QK_HEAD)
    wq = jnp.concatenate([_pad_heads(wq_h.reshape(d, Q_LORA, -1), MLA_HEADS, QK_HEAD),
                          _pad_heads(_swap_rope_pairs(wq_h).reshape(d, Q_LORA, -1), MLA_HEADS, QK_HEAD)],
                         axis=-1)
    kvu = w_kv_up.reshape(d, KV_LORA, MLA_HEADS, QK_NOPE + V_HEAD)
    wk = _pad_heads(kvu[..., :QK_NOPE].reshape(d, KV_LORA, -1), MLA_HEADS, QK_NOPE)
    wv = kvu[..., QK_NOPE:].reshape(d, KV_LORA, MLA_W)
    lane_pad = ((0, 0), (0, HEAD_PAD - QK_HEAD))
    qhg = jnp.stack([jnp.pad(q_head_norm_g, lane_pad),
                     jnp.pad(_swap_rope_pairs(q_head_norm_g), lane_pad)], axis=1)
    return {
        "g1": norm1_g.reshape(d, 1, D_MODEL),
        "g2": norm2_g.reshape(d, 1, D_MODEL),
        "w_in": _w_in_layout_call(w_in),
        "qng": q_norm_g.reshape(d, 1, Q_LORA),
        "kvng": kv_norm_g.reshape(d, 1, KV_LORA),
        "wq": wq.astype(BF16),
        "wkv": jnp.concatenate([wk, wv], axis=-1).astype(BF16),
        "qhg": jnp.pad(qhg, ((0, 0), (0, SUBLANES - 2), (0, 0))),
        "khg": jnp.pad(k_head_norm_g, lane_pad).reshape(d, 1, LANES),
        "convw": jnp.pad(conv_w, ((0, 0), (0, SUBLANES - 3), (0, 0))),
        "wo": w_o.astype(BF16),
        "wg": w_ffn_gate.astype(BF16),
        "wu": w_ffn_up.astype(BF16),
        "wd": w_ffn_down.astype(BF16),
    }


def _rope_tables(t):
    half = QK_ROPE // 2
    freqs = jnp.power(ROPE_THETA, -jnp.arange(0, half, 2, dtype=F32) / half)
    lane = jnp.arange(LANES)
    rel = lane - QK_NOPE
    in_rope = (rel >= 0) & (rel < QK_ROPE)
    freq_lane = freqs[jnp.clip(rel, 0, QK_ROPE - 1) % ROPE_PAIR]
    first = in_rope & ((rel % half) < ROPE_PAIR)
    second = in_rope & ((rel % half) >= ROPE_PAIR)

    def stack(n_pos):
        ang = jnp.arange(n_pos, dtype=F32)[:, None] * freq_lane[None, :]
        sin = jnp.sin(ang)
        return jnp.stack([jnp.where(in_rope[None, :], jnp.cos(ang), 1.0),
                          jnp.where(first[None, :], -sin, 0.0),
                          jnp.where(second[None, :], sin, 0.0)])

    return stack(t // GRID_W), stack(GRID_W)


def _decay_lanes(ret_decay_fwd, ret_decay_bwd):
    lg = jnp.stack([jax.nn.log_sigmoid(ret_decay_fwd.astype(F32)),
                    jax.nn.log_sigmoid(ret_decay_bwd.astype(F32))], axis=1)
    pad = ((0, 0), (0, SUBLANES - 2), (0, 0))
    return (jnp.pad(jnp.repeat(lg, RET_DV, axis=2), pad),
            jnp.pad(jnp.repeat(lg, RET_CHUNK, axis=2), pad))


TM = 1024
PROJ_SUB = 256
PROJ_CONV_SLOT = 1
TM_OUT = 1024
OUT_SUB = 256
RET_GROUP = 8
RET_GROUP_LATENT = 2
TQ = 2048
TQ_UNIT = 128
CTX_ATTN_LOOKAHEAD = 2


def kernel(x_prompt, x_sample, cache_ckv, cache_kpe, state_ret, c, c_ctx, ada_w, ada_b, norm1_g, norm2_g, w_in, q_norm_g, kv_norm_g, w_q_up, w_kv_up, q_head_norm_g, k_head_norm_g, conv_w, ret_decay_fwd, ret_decay_bwd, w_o, w_ffn_gate, w_ffn_up, w_ffn_down):
    bp, tp, _ = x_prompt.shape
    bs, ts, _ = x_sample.shape

    cond = jnp.concatenate([c_ctx[None, :], c, jnp.zeros((SUBLANES - 1 - bs, D_MODEL), F32)], axis=0)
    mods = _ada_call(cond, ada_w, ada_b).reshape(DEPTH, SUBLANES, 1, 6 * D_MODEL)

    pw = _prep_weights(w_in, q_norm_g, kv_norm_g, w_q_up, w_kv_up, q_head_norm_g, k_head_norm_g,
                       conv_w, norm1_g, norm2_g, w_o, w_ffn_gate, w_ffn_up, w_ffn_down)
    rope_tabs = _rope_tables(ts)
    lgl, lgc = _decay_lanes(ret_decay_fwd, ret_decay_bwd)
    kpe_ctx = jnp.pad(cache_kpe, ((0, 0), (0, 0), (0, 0), (QK_NOPE, LANES - QK_HEAD)))
    kc, vc = _ctx_kv_call(cache_ckv, kpe_ctx, pw["wkv"], pw["khg"])

    xp = x_prompt.reshape(bp * tp, D_MODEL)
    xs = x_sample.reshape(bs * ts, D_MODEL)
    caches, new_state = None, None
    for l in range(DEPTH):
        attn, *caches, conv, rq, rk, rv, rg = _proj_call(
            xp, mods, pw, l, None, caches, latent=False, seq=tp, tm=TM)
        r3 = lambda a: a.reshape(bp, tp, RET_W)
        ret, new_state = _ret_call(lgl, lgc, r3(rq), r3(rk), r3(rv), r3(rg), new_state, l,
                                   latent=False, group=RET_GROUP)
        xp = _out_call(xp, attn, conv, ret.reshape(bp * tp, RET_W), mods, pw, l,
                       latent=False, seq=tp, tm=TM_OUT)

        q, k, v, conv, rq, rk, rv, rg = _proj_call(
            xs, mods, pw, l, rope_tabs, (), latent=True, seq=ts, tm=TM)
        attn = _attn_call(q.reshape(bs, ts, QK_W), k.reshape(bs, ts, QK_W),
                          v.reshape(bs, ts, MLA_W), kc, vc, l, tq=TQ, unit=TQ_UNIT)
        r3 = lambda a: a.reshape(bs, ts, RET_W)
        (ret,) = _ret_call(lgl, lgc, r3(rq), r3(rk), r3(rv), r3(rg), state_ret, l,
                           latent=True, group=RET_GROUP_LATENT)
        xs = _out_call(xs, attn.reshape(bs * ts, MLA_W), conv, ret.reshape(bs * ts, RET_W),
                       mods, pw, l, latent=True, seq=ts, tm=TM_OUT)

    return (xp.reshape(bp, tp, D_MODEL), xs.reshape(bs, ts, D_MODEL),
            caches[0], jnp.swapaxes(caches[1], 2, 3), new_state)
```

```python
import functools

import jax
import jax.numpy as jnp
from jax import lax
from jax.experimental import pallas as pl
from jax.experimental.pallas import tpu as pltpu

D_MODEL = 1024
DEPTH = 2
GRID_W = 64
MLA_HEADS = 8
Q_LORA = 256
KV_LORA = 128
QK_NOPE = 64
QK_ROPE = 32
V_HEAD = 64
QK_HEAD = QK_NOPE + QK_ROPE
MLA_W = MLA_HEADS * V_HEAD
CONV_W = 256
RET_HEADS = 4
RET_DK = 64
RET_DV = 64
RET_W = RET_HEADS * RET_DV
RET_CHUNK = 128
FFN_HIDDEN = 2816
ROPE_THETA = 10000.0
EPS = 1e-6
LOG2_E = 1.4426950408889634

LANES = 128
SUBLANES = 8
MXU_N = 256
HEAD_PAD = LANES
QK_W = MLA_HEADS * HEAD_PAD
ROPE_PAIR = QK_ROPE // 4
VMEM_LIMIT = 56 * 1024 * 1024

C_QLAT = 0
C_KVLAT = C_QLAT + Q_LORA
C_KPE = C_KVLAT + KV_LORA
C_GB = C_KPE + LANES
C_GC = C_GB + CONV_W
C_XIN = C_GC + CONV_W
C_RQ = C_XIN + CONV_W
C_RK = C_RQ + RET_W
C_RV = C_RK + RET_W
C_RG = C_RV + RET_W
IN_COLS_PAD = C_RG + RET_W

BF16 = jnp.bfloat16
F32 = jnp.float32


def _dot(a, b):
    return jnp.dot(a, b, preferred_element_type=F32)


def _dot_nt(a, b):
    return lax.dot_general(a, b, (((1,), (1,)), ((), ())), preferred_element_type=F32)


def _rms(x, n=None):
    n = x.shape[-1] if n is None else n
    return x * lax.rsqrt(jnp.sum(x * x, axis=-1, keepdims=True) * (1.0 / n) + EPS)


def _silu(x):
    return x * (1.0 / (1.0 + jnp.exp(-x)))


def _const_spec(shape):
    nd = len(shape)
    return pl.BlockSpec(shape, lambda *_: (0,) * nd, pipeline_mode=pl.Buffered(1))


def _layer_spec(shape, layer):
    nd = len(shape)
    return pl.BlockSpec((None,) + tuple(shape), lambda *_: (layer,) + (0,) * nd,
                        pipeline_mode=pl.Buffered(1))


def _softmax_pv(s_list, v_list):
    m = functools.reduce(jnp.maximum, [jnp.max(s, axis=-1, keepdims=True) for s in s_list])
    l, o = None, None
    for s, v in zip(s_list, v_list):
        p = jnp.exp2(s - m)
        ls = jnp.sum(p, axis=-1, keepdims=True)
        os_ = _dot(p.astype(BF16), v)
        l = ls if l is None else l + ls
        o = os_ if o is None else o + os_
    return o * (1.0 / l)


def _merge_head_pair(o0, o1):
    lane = lax.broadcasted_iota(jnp.int32, o0.shape, 1)
    return jnp.where(lane < V_HEAD, o0, o1).astype(BF16)


ADA_TN = 1536


def _ada_kernel(cond_ref, w_ref, b_ref, o_ref):
    a = _silu(cond_ref[...]).astype(BF16)
    o_ref[...] = _dot(a, w_ref[...].astype(BF16)) + b_ref[...]


def _ada_call(cond, ada_w, ada_b):
    n = 6 * D_MODEL
    return pl.pallas_call(
        _ada_kernel,
        grid=(DEPTH, n // ADA_TN),
        in_specs=[
            pl.BlockSpec((SUBLANES, D_MODEL), lambda l, j: (0, 0)),
            pl.BlockSpec((None, D_MODEL, ADA_TN), lambda l, j: (l, 0, j)),
            pl.BlockSpec((None, 1, ADA_TN), lambda l, j: (l, 0, j)),
        ],
        out_specs=pl.BlockSpec((None, SUBLANES, ADA_TN), lambda l, j: (l, 0, j)),
        out_shape=jax.ShapeDtypeStruct((DEPTH, SUBLANES, n), F32),
        compiler_params=pltpu.CompilerParams(
            dimension_semantics=("arbitrary", "arbitrary"), vmem_limit_bytes=VMEM_LIMIT),
        name="ada_mod",
    )(cond, ada_w, ada_b.reshape(DEPTH, 1, n))


def _key_slab(kn, shared, pe_ssq, g):
    ssq = jnp.sum(kn * kn, axis=-1, keepdims=True) + pe_ssq
    return ((kn * g + shared) * lax.rsqrt(ssq * (1.0 / QK_HEAD) + EPS)).astype(BF16)


def _ctx_kv_kernel(ckv_ref, kpe_ref, wkv_ref, khg_ref, k_ref, v_ref):
    g = khg_ref[...]
    for i in range(ckv_ref.shape[0]):
        kv = _dot(ckv_ref[i].astype(BF16), wkv_ref[...])
        kpe = kpe_ref[i]
        shared = kpe * g
        pe_ssq = jnp.sum(kpe * kpe, axis=-1, keepdims=True)
        for hd in range(MLA_HEADS):
            sl = slice(hd * HEAD_PAD, (hd + 1) * HEAD_PAD)
            k_ref[i, :, sl] = _key_slab(kv[:, sl], shared, pe_ssq, g)
        v_ref[i] = kv[:, QK_W:].astype(BF16)


def _ctx_kv_call(cache_ckv, kpe_slab, wkv, khg):
    b, _, t, _ = cache_ckv.shape
    return pl.pallas_call(
        _ctx_kv_kernel,
        grid=(DEPTH,),
        in_specs=[
            pl.BlockSpec((b, None, t, KV_LORA), lambda l: (0, l, 0, 0)),
            pl.BlockSpec((b, None, t, LANES), lambda l: (0, l, 0, 0)),
            pl.BlockSpec((None, KV_LORA, QK_W + MLA_W), lambda l: (l, 0, 0)),
            pl.BlockSpec((None, 1, LANES), lambda l: (l, 0, 0)),
        ],
        out_specs=[
            pl.BlockSpec((None, b, t, QK_W), lambda l: (l, 0, 0, 0)),
            pl.BlockSpec((None, b, t, MLA_W), lambda l: (l, 0, 0, 0)),
        ],
        out_shape=[
            jax.ShapeDtypeStruct((DEPTH, b, t, QK_W), BF16),
            jax.ShapeDtypeStruct((DEPTH, b, t, MLA_W), BF16),
        ],
        compiler_params=pltpu.CompilerParams(
            dimension_semantics=("arbitrary",), vmem_limit_bytes=VMEM_LIMIT),
        name="ctx_kv",
    )(cache_ckv, kpe_slab, wkv, khg)


def _store_layer_slot(ref, idx, layer, n_alias, value):
    if n_alias:
        ref[idx] = value
    else:
        for l in range(ref.shape[1]):
            ref[idx, l] = value if l == layer else jnp.zeros_like(value)


def _proj_kernel(*refs, latent, tiles_per_seq, tm, layer, n_alias):
    it = iter(refs)
    x_ref = next(it)
    if latent:
        xp_ref, xn_ref = next(it), next(it)
    mod_ref, g1_ref, win_ref, qng_ref, kvng_ref = (next(it) for _ in range(5))
    wq_ref, wkv_ref, qhg_ref, khg_ref, cw_ref = (next(it) for _ in range(5))
    if latent:
        rope_row_ref, rope_col_ref = next(it), next(it)
        q_out, k_out, v_out = next(it), next(it), next(it)
    else:
        for _ in range(n_alias):
            next(it)
        attn_out, ckv_out, kpe_out = next(it), next(it), next(it)
    conv_out, rq_out, rk_out, rv_out, rg_out = (next(it) for _ in range(5))

    shift = mod_ref[:, 0:D_MODEL]
    scale1 = 1.0 + mod_ref[:, D_MODEL:2 * D_MODEL]
    g1 = g1_ref[...]

    def norm_mod(x):
        return ((_rms(x) * g1) * scale1 + shift).astype(BF16)

    q_scale = QK_HEAD ** -0.5 * LOG2_E
    qg = qhg_ref[0:1, :] * q_scale
    kg = khg_ref[...]
    head_slice = lambda hd: slice(hd * HEAD_PAD, (hd + 1) * HEAD_PAD)
    n_sub = tm // PROJ_SUB
    subs = [slice(t * PROJ_SUB, (t + 1) * PROJ_SUB) for t in range(n_sub)]
    state = [dict() for _ in subs]

    def rope_tables(t):
        lane = lax.broadcasted_iota(jnp.int32, (GRID_W, LANES), 1)
        row_lane = (lane >= QK_NOPE) & (lane < QK_NOPE + QK_ROPE // 2)
        tile_in_seq = pl.program_id(0) % tiles_per_seq
        grid_row0 = tile_in_seq * (tm // GRID_W) + t * (PROJ_SUB // GRID_W)
        tabs = []
        for k in range(3):
            blocks = [jnp.where(row_lane, rope_row_ref[k, pl.ds(grid_row0 + b, 1), :], rope_col_ref[k])
                      for b in range(PROJ_SUB // GRID_W)]
            tabs.append(jnp.concatenate(blocks, axis=0))
        return tabs

    def projection_stage(t):
        rows, st = subs[t], state[t]

        def proj(col, width):
            return _dot(st["h"], win_ref[:, col:col + width])

        def latents():
            st["h"] = norm_mod(x_ref[rows, :])
            st["q_lat"] = proj(C_QLAT, Q_LORA)
            st["kv_pe"] = proj(C_KVLAT, KV_LORA + LANES)

        def conv_inputs():
            st["u"] = proj(C_GC, CONV_W) * proj(C_XIN, CONV_W)
            if latent and t == 0:
                hh = norm_mod(jnp.concatenate([xp_ref[...], xn_ref[...]], axis=0))
                halo = (_dot(hh, win_ref[:, C_GC:C_GC + CONV_W])
                        * _dot(hh, win_ref[:, C_XIN:C_XIN + CONV_W]))
                i = pl.program_id(0)
                has_prev = jnp.where(i % tiles_per_seq != 0, 1.0, 0.0)
                has_next = jnp.where(i % tiles_per_seq != tiles_per_seq - 1, 1.0, 0.0)
                state[0]["u_before"] = halo[SUBLANES - 1:SUBLANES, :] * has_prev
                state[-1]["u_after"] = halo[SUBLANES:SUBLANES + 1, :] * has_next

        def queries():
            qn = (_rms(st["q_lat"]) * qng_ref[...]).astype(BF16)
            st["q"] = _dot(qn, wq_ref[:, 0:QK_W])
            if latent:
                st["q_sw"] = _dot(qn, wq_ref[:, QK_W:2 * QK_W])
                cos, sin_lo, sin_hi = st["rope"] = rope_tables(t)
                st["q_tab"] = qg * cos
                st["q_tab_sw"] = (qhg_ref[1:2, :] * q_scale) * (sin_lo + sin_hi)

        def keys_values():
            ckv = _rms(st["kv_pe"][:, 0:KV_LORA]) * kvng_ref[...]
            kpe = st["kv_pe"][:, KV_LORA:]
            st["kv"] = _dot(ckv.astype(BF16), wkv_ref[...])
            shared = kpe * kg
            if latent:
                cos, sin_lo, sin_hi = st["rope"]
                shared = (shared * cos + pltpu.roll(shared, LANES - ROPE_PAIR, axis=1) * sin_lo
                          + pltpu.roll(shared, ROPE_PAIR, axis=1) * sin_hi)
            else:
                _store_layer_slot(ckv_out, t, layer, n_alias, ckv)
                _store_layer_slot(kpe_out, t, layer, n_alias, kpe.T[QK_NOPE:QK_HEAD, :])
            st["shared"] = shared
            st["pe_ssq"] = jnp.sum(kpe * kpe, axis=-1, keepdims=True)

        def conv_gate():
            st["gb"] = proj(C_GB, CONV_W)

        def ret_q():
            rq_out[rows, :] = proj(C_RQ, RET_W).astype(BF16)

        def ret_k():
            rk_out[rows, :] = (proj(C_RK, RET_W) * (RET_DK ** -0.5)).astype(BF16)

        def ret_v():
            rv_out[rows, :] = proj(C_RV, RET_W).astype(BF16)

        def ret_g():
            rg_out[rows, :] = proj(C_RG, RET_W)

        return [latents, conv_inputs, queries, keys_values, conv_gate, ret_q, ret_k, ret_v, ret_g]

    def conv_finish(t):
        st, u = state[t], state[t]["u"]
        zero = jnp.zeros((1, CONV_W), F32)
        if latent:
            u_before = st["u_before"] if t == 0 else state[t - 1]["u"][PROJ_SUB - 1:PROJ_SUB, :]
            u_after = st["u_after"] if t == n_sub - 1 else state[t + 1]["u"][0:1, :]
        else:
            u_before, u_after = zero, zero
        row = lax.broadcasted_iota(jnp.int32, (PROJ_SUB, CONV_W), 0)
        u_m1 = jnp.where(row == 0, u_before, pltpu.roll(u, 1, axis=0))
        u_p1 = jnp.where(row == PROJ_SUB - 1, u_after, pltpu.roll(u, PROJ_SUB - 1, axis=0))
        y = u_m1 * cw_ref[0:1, :] + u * cw_ref[1:2, :] + u_p1 * cw_ref[2:3, :]
        conv_out[subs[t], :] = (st["gb"] * y).astype(BF16)

    def head_stage(t):
        rows, st = subs[t], state[t]

        def query_slab(sl):
            qs = st["q"][:, sl]
            r = lax.rsqrt(jnp.sum(qs * qs, axis=-1, keepdims=True) * (1.0 / QK_HEAD) + EPS)
            if latent:
                return ((qs * st["q_tab"] + st["q_sw"][:, sl] * st["q_tab_sw"]) * r).astype(BF16)
            return (qs * qg * r).astype(BF16)

        def key_slab(sl):
            return _key_slab(st["kv"][:, sl], st["shared"], st["pe_ssq"], kg)

        if latent:
            def store_head(hd):
                sl = head_slice(hd)
                q_out[rows, sl] = query_slab(sl)
                k_out[rows, sl] = key_slab(sl)
                if hd == MLA_HEADS - 1:
                    v_out[rows, :] = st["kv"][:, QK_W:].astype(BF16)
            return [functools.partial(store_head, hd) for hd in range(MLA_HEADS)]

        ahead, pair = [], []

        def scores(hd):
            sl = head_slice(hd)
            return _dot_nt(query_slab(sl), key_slab(sl))

        def attend(hd):
            if hd == 0:
                ahead.extend(scores(j) for j in range(CTX_ATTN_LOOKAHEAD))
            s = ahead.pop(0)
            if hd + CTX_ATTN_LOOKAHEAD < MLA_HEADS:
                ahead.append(scores(hd + CTX_ATTN_LOOKAHEAD))
            hp = hd // 2
            v_pair = st["kv"][:, QK_W + hp * LANES:QK_W + (hp + 1) * LANES].astype(BF16)
            pair.append(_softmax_pv([s], [v_pair]))
            if hd % 2 == 1:
                attn_out[rows, hp * LANES:(hp + 1) * LANES] = _merge_head_pair(*pair)
                pair.clear()

        return [functools.partial(attend, hd) for hd in range(MLA_HEADS)]

    for piece in projection_stage(0):
        piece()
    for t in range(n_sub):
        heads = head_stage(t)
        nxt = projection_stage(t + 1) if t + 1 < n_sub else []
        for i in range(max(len(heads), len(nxt))):
            if i < len(nxt):
                nxt[i]()
            if i < len(heads):
                heads[i]()
            if i == PROJ_CONV_SLOT and t > 0:
                conv_finish(t - 1)
    conv_finish(n_sub - 1)


def _proj_call(x2d, mod, pw, layer, rope_tabs, cache_bufs, *, latent, seq, tm):
    rows = x2d.shape[0]
    tiles_per_seq = seq // tm
    n_tiles = rows // tm
    if latent:
        mod_map = lambda i: (layer, 1 + i // tiles_per_seq, 0, 0)
    else:
        assert seq == PROJ_SUB, "the context path runs one sequence's attention per sub-tile"
        mod_map = lambda i: (layer, 0, 0, 0)
    row_spec = lambda w: pl.BlockSpec((tm, w), lambda i: (i, 0))
    in_specs = [row_spec(D_MODEL)]
    args = [x2d]
    if latent:
        blk = tm // SUBLANES
        last = rows // SUBLANES - 1
        in_specs += [
            pl.BlockSpec((SUBLANES, D_MODEL), lambda i: (jnp.maximum(i * blk - 1, 0), 0)),
            pl.BlockSpec((SUBLANES, D_MODEL), lambda i: (jnp.minimum((i + 1) * blk, last), 0)),
        ]
        args += [x2d, x2d]
    in_specs += [
        pl.BlockSpec((None, None, 1, 6 * D_MODEL), mod_map),
        _layer_spec((1, D_MODEL), layer),
        _layer_spec((D_MODEL, IN_COLS_PAD), layer),
        _layer_spec((1, Q_LORA), layer),
        _layer_spec((1, KV_LORA), layer),
        _layer_spec((Q_LORA, 2 * QK_W), layer),
        _layer_spec((KV_LORA, QK_W + MLA_W), layer),
        _layer_spec((SUBLANES, LANES), layer),
        _layer_spec((1, LANES), layer),
        _layer_spec((SUBLANES, CONV_W), layer),
    ]
    args += [mod, pw["g1"], pw["w_in"], pw["qng"], pw["kvng"], pw["wq"], pw["wkv"],
             pw["qhg"], pw["khg"], pw["convw"]]
    if latent:
        in_specs += [_const_spec(tab.shape) for tab in rope_tabs]
        args += list(rope_tabs)
        outs = [(QK_W, BF16), (QK_W, BF16), (MLA_W, BF16)]
        aliases = {}
    else:
        outs = [(MLA_W, BF16)]
    out_specs = [row_spec(w) for w, _ in outs]
    out_shape = [jax.ShapeDtypeStruct((rows, w), dt) for w, dt in outs]
    n_alias = 0
    if not latent:
        n_seq = tm // seq
        batch = rows // seq
        shapes = [(batch, DEPTH, seq, KV_LORA), (batch, DEPTH, QK_ROPE, seq)]
        if cache_bufs is None:
            aliases = {}
            out_specs += [pl.BlockSpec((n_seq,) + shp[1:], lambda i: (i, 0, 0, 0)) for shp in shapes]
        else:
            n_alias = len(cache_bufs)
            aliases = {len(args) + j: len(outs) + j for j in range(n_alias)}
            in_specs += [pl.BlockSpec(memory_space=pl.ANY)] * n_alias
            args += list(cache_bufs)
            out_specs += [pl.BlockSpec((n_seq, None) + shp[2:], lambda i: (i, layer, 0, 0))
                          for shp in shapes]
        out_shape += [jax.ShapeDtypeStruct(shp, F32) for shp in shapes]
    tail = [(CONV_W, BF16), (RET_W, BF16), (RET_W, BF16), (RET_W, BF16), (RET_W, F32)]
    out_specs += [row_spec(w) for w, _ in tail]
    out_shape += [jax.ShapeDtypeStruct((rows, w), dt) for w, dt in tail]
    return pl.pallas_call(
        functools.partial(_proj_kernel, latent=latent, tiles_per_seq=tiles_per_seq, tm=tm,
                          layer=layer, n_alias=n_alias),
        grid=(n_tiles,),
        in_specs=in_specs,
        out_specs=out_specs,
        out_shape=out_shape,
        input_output_aliases=aliases,
        compiler_params=pltpu.CompilerParams(
            dimension_semantics=("arbitrary",), vmem_limit_bytes=VMEM_LIMIT),
        name="proj_latent" if latent else "proj_ctx",
    )(*args)


def _attn_kernel(q_ref, k_ref, v_ref, kc_ref, vc_ref, o_ref, p_ref, pc_ref, *, unit):
    units = [(r0, j) for r0 in range(0, q_ref.shape[0], unit) for j in range(2)]
    n = len(units)

    def scores(u):
        r0, j = units[u]
        sl = slice(j * HEAD_PAD, (j + 1) * HEAD_PAD)
        q = q_ref[r0:r0 + unit, sl]
        return _dot_nt(q, k_ref[:, sl]), _dot_nt(q, kc_ref[:, sl])

    def softmax(u, s, sc):
        m = jnp.maximum(jnp.max(s, axis=-1, keepdims=True), jnp.max(sc, axis=-1, keepdims=True))
        p, pc = jnp.exp2(s - m), jnp.exp2(sc - m)
        p_ref[u % 2] = p.astype(BF16)
        pc_ref[u % 2] = pc.astype(BF16)
        return jnp.sum(p, axis=-1, keepdims=True) + jnp.sum(pc, axis=-1, keepdims=True)

    def values(u, l):
        o = _dot(p_ref[u % 2], v_ref[...]) + _dot(pc_ref[u % 2], vc_ref[...])
        return o * (1.0 / l)

    s_ahead = {0: scores(0), 1: scores(1)}
    l_ahead = {0: softmax(0, *s_ahead.pop(0))}
    pair = []
    for u, (r0, j) in enumerate(units):
        if u + 2 < n:
            s_ahead[u + 2] = scores(u + 2)
        if u + 1 < n:
            l_ahead[u + 1] = softmax(u + 1, *s_ahead.pop(u + 1))
        pair.append(values(u, l_ahead.pop(u)))
        if j == 1:
            o_ref[r0:r0 + unit, :] = _merge_head_pair(*pair)
            pair = []


def _attn_call(q, k, v, kc, vc, layer, *, tq, unit):
    b, t, _ = q.shape
    tk, tc = k.shape[1], kc.shape[2]
    return pl.pallas_call(
        functools.partial(_attn_kernel, unit=unit),
        grid=(b, MLA_HEADS // 2, t // tq),
        in_specs=[
            pl.BlockSpec((None, tq, 2 * HEAD_PAD), lambda bi, hp, i: (bi, i, hp)),
            pl.BlockSpec((None, tk, 2 * HEAD_PAD), lambda bi, hp, i: (bi, 0, hp)),
            pl.BlockSpec((None, tk, 2 * V_HEAD), lambda bi, hp, i: (bi, 0, hp)),
            pl.BlockSpec((None, None, tc, 2 * HEAD_PAD), lambda bi, hp, i: (layer, bi, 0, hp)),
            pl.BlockSpec((None, None, tc, 2 * V_HEAD), lambda bi, hp, i: (layer, bi, 0, hp)),
        ],
        out_specs=pl.BlockSpec((None, tq, 2 * V_HEAD), lambda bi, hp, i: (bi, i, hp)),
        out_shape=jax.ShapeDtypeStruct((b, t, MLA_W), BF16),
        scratch_shapes=[pltpu.VMEM((2, unit, tk), BF16), pltpu.VMEM((2, unit, tc), BF16)],
        compiler_params=pltpu.CompilerParams(
            dimension_semantics=("arbitrary", "arbitrary", "arbitrary"),
            vmem_limit_bytes=VMEM_LIMIT),
        name="attn_latent",
    )(q, k, v, kc, vc)


def _ret_kernel(*refs, latent, n_chunks, group, layer, n_alias):
    if latent:
        lgl_ref, lgc_ref, rq_ref, rk_ref, rv_ref, rg_ref, s0_ref, out_ref, o_acc, st_ref = refs
    else:
        lgl_ref, lgc_ref, rq_ref, rk_ref, rv_ref, rg_ref = refs[:6]
        out_ref, sfin_ref, o_acc, st_ref = refs[6 + n_alias:]
    c = RET_CHUNK
    seqs = range(group)
    lg_f, lg_b = lgl_ref[0:1, :], lgl_ref[1:2, :]
    pos = lax.broadcasted_iota(jnp.int32, (c, RET_W), 0).astype(F32)
    qdec_f = jnp.exp(lg_f * (pos + 1.0)).astype(BF16)
    kdec_f = jnp.exp(lg_f * (c - 1.0 - pos)).astype(BF16)
    qdec_b = jnp.exp(lg_b * (c - pos)).astype(BF16)
    kdec_b = jnp.exp(lg_b * pos).astype(BF16)
    cdec_f = jnp.exp(lg_f * float(c))
    cdec_b = jnp.exp(lg_b * float(c))
    ri = lax.broadcasted_iota(jnp.int32, (c, RET_HEADS * c), 0)
    ci = jnp.bitwise_and(lax.broadcasted_iota(jnp.int32, (c, RET_HEADS * c), 1), c - 1)
    diff = (ri - ci).astype(F32)
    dcat = (jnp.where(diff >= 0, jnp.exp(lgc_ref[0:1, :] * diff), 0.0)
            + jnp.where(diff <= 0, jnp.exp(lgc_ref[1:2, :] * (-diff)), 0.0))
    head_shift = RET_DV.bit_length() - 1
    lane_head = jnp.right_shift(lax.broadcasted_iota(jnp.int32, (c, RET_W), 1), head_shift)
    r_head = jnp.right_shift(lax.broadcasted_iota(jnp.int32, (RET_W, RET_W), 0), head_shift)
    c_head = jnp.right_shift(lax.broadcasted_iota(jnp.int32, (RET_W, RET_W), 1), head_shift)
    same_head = r_head == c_head
    head_mean = jnp.where(same_head, 1.0 / RET_DV, 0.0).astype(BF16)

    def stack_heads(a):
        zero = jnp.zeros_like(a)
        return jnp.concatenate([jnp.where(lane_head == hd, a, zero) for hd in range(RET_HEADS)], axis=0)

    def chunk(n):
        return n * c if isinstance(n, int) else pl.multiple_of(n * c, c)

    def load(sl):
        return ([rq_ref[g, sl, :] for g in seqs], [rk_ref[g, sl, :] for g in seqs],
                [rv_ref[g, sl, :] for g in seqs])

    def state_terms(d, qs, ks, vs, qdec, kdec):
        inc = [_dot((k * kdec).T, v) for k, v in zip(ks, vs)]
        zero = jnp.zeros((RET_W, RET_W), BF16)
        inter = [_dot(q * qdec, jnp.where(same_head, st_ref[g, d].astype(BF16), zero))
                 for g, q in zip(seqs, qs)]
        return inter, inc

    def state_update(d, inc, cdec):
        for g in seqs:
            st_ref[g, d] = st_ref[g, d] * cdec + inc[g]

    def finish(sl, os_):
        sq = [o * o for o in os_]
        hi = [x.astype(BF16) for x in sq]
        ms = [_dot(h, head_mean) + _dot((x - h.astype(F32)).astype(BF16), head_mean)
              for x, h in zip(sq, hi)]
        for g in seqs:
            y = os_[g] * lax.rsqrt(ms[g] + EPS)
            out_ref[g, sl, :] = (y * _silu(rg_ref[g, sl, :])).astype(BF16)

    def step(j, second_half):
        sl_f = pl.ds(chunk(j), c)
        sl_b = pl.ds(chunk(n_chunks - 1 - j), c)
        qf, kf, vf = load(sl_f)
        qb, kb, vb = load(sl_b)
        scores = [_dot_nt(q, stack_heads(k)) for q, k in zip(qf, kf)]
        inter_f, inc_f = state_terms(0, qf, kf, vf, qdec_f, kdec_f)
        inter_b, inc_b = state_terms(1, qb, kb, vb, qdec_b, kdec_b)
        o_f = [_dot((scores[g] * dcat).astype(BF16), stack_heads(vf[g])) + inter_f[g] for g in seqs]
        state_update(0, inc_f, cdec_f)
        state_update(1, inc_b, cdec_b)
        if second_half:
            finish(sl_f, [o_acc[g, sl_f, :] + o_f[g] for g in seqs])
            finish(sl_b, [o_acc[g, sl_b, :] + inter_b[g] for g in seqs])
        else:
            for g in seqs:
                o_acc[g, sl_f, :] = o_f[g]
                o_acc[g, sl_b, :] = inter_b[g]

    def scan(lo, hi, second_half):
        if hi - lo <= 1:
            for j in range(lo, hi):
                step(j, second_half)
        else:
            lax.fori_loop(lo, hi, lambda j, carry: (step(j, second_half), carry)[1], 0)

    st_ref[...] = jnp.zeros_like(st_ref)
    if latent:
        for g in seqs:
            for d in range(2):
                for hd in range(RET_HEADS):
                    st_ref[g, d, hd * RET_DK:(hd + 1) * RET_DK, hd * RET_DV:(hd + 1) * RET_DV] = s0_ref[g, d, hd]
    half = n_chunks // 2
    scan(0, half, False)
    scan(half, n_chunks, True)
    if not latent:
        for g in seqs:
            for d in range(2):
                for hd in range(RET_HEADS):
                    block = st_ref[g, d, hd * RET_DK:(hd + 1) * RET_DK, hd * RET_DV:(hd + 1) * RET_DV]
                    if n_alias:
                        sfin_ref[g, d, hd] = block
                    else:
                        for l in range(sfin_ref.shape[1]):
                            sfin_ref[g, l, d, hd] = block if l == layer else jnp.zeros_like(block)


def _ret_call(lgl, lgc, rq, rk, rv, rg, state_ret, layer, *, latent, group):
    b, t, _ = rq.shape
    assert (t // RET_CHUNK) % 2 == 0, "the two scans meet in the middle of an even chunk count"
    seq_spec = pl.BlockSpec((group, t, RET_W), lambda i: (i, 0, 0))
    st_block = (2, RET_HEADS, RET_DK, RET_DV)
    in_specs = [_layer_spec((SUBLANES, RET_W), layer),
                _layer_spec((SUBLANES, RET_HEADS * RET_CHUNK), layer),
                seq_spec, seq_spec, seq_spec, seq_spec]
    args = [lgl, lgc, rq, rk, rv, rg]
    out_specs = [seq_spec]
    out_shape = [jax.ShapeDtypeStruct((b, t, RET_W), BF16)]
    st_spec = pl.BlockSpec((group, None) + st_block, lambda i: (i, layer, 0, 0, 0, 0))
    aliases, n_alias = {}, 0
    if latent:
        in_specs.append(st_spec)
        args.append(state_ret)
    elif state_ret is None:
        out_specs.append(pl.BlockSpec((group, DEPTH) + st_block, lambda i: (i, 0, 0, 0, 0, 0)))
        out_shape.append(jax.ShapeDtypeStruct((b, DEPTH) + st_block, F32))
    else:
        in_specs.append(pl.BlockSpec(memory_space=pl.ANY))
        aliases, n_alias = {len(args): 1}, 1
        args.append(state_ret)
        out_specs.append(st_spec)
        out_shape.append(jax.ShapeDtypeStruct(state_ret.shape, state_ret.dtype))
    return pl.pallas_call(
        functools.partial(_ret_kernel, latent=latent, n_chunks=t // RET_CHUNK, group=group,
                          layer=layer, n_alias=n_alias),
        grid=(b // group,),
        in_specs=in_specs,
        out_specs=out_specs,
        out_shape=out_shape,
        input_output_aliases=aliases,
        scratch_shapes=[pltpu.VMEM((group, t, RET_W), F32),
                        pltpu.VMEM((group, 2, RET_W, RET_W), F32)],
        compiler_params=pltpu.CompilerParams(
            dimension_semantics=("arbitrary",), vmem_limit_bytes=VMEM_LIMIT),
        name="ret_latent" if latent else "ret_ctx",
    )(*args)


def _out_kernel(x_ref, attn_ref, conv_ref, ret_ref, mod_ref, g2_ref, wo_ref, wg_ref, wu_ref,
                wd_ref, o_ref, act_ref):
    subs = [slice(r0, r0 + OUT_SUB) for r0 in range(0, x_ref.shape[0], OUT_SUB)]
    x1 = []
    for rows in subs:
        mix = (_dot(attn_ref[rows, :], wo_ref[0:MLA_W, :])
               + _dot(conv_ref[rows, :], wo_ref[MLA_W:MLA_W + CONV_W, :])
               + _dot(ret_ref[rows, :], wo_ref[MLA_W + CONV_W:, :]))
        x1.append(x_ref[rows, :] + mod_ref[:, 2 * D_MODEL:3 * D_MODEL] * mix)
    for rows, xr in zip(subs, x1):
        h = ((_rms(xr) * g2_ref[...]) * (1.0 + mod_ref[:, 4 * D_MODEL:5 * D_MODEL])
             + mod_ref[:, 3 * D_MODEL:4 * D_MODEL]).astype(BF16)
        for j in range(FFN_HIDDEN // MXU_N):
            sl = slice(j * MXU_N, (j + 1) * MXU_N)
            gate = _dot(h, wg_ref[:, sl])
            up = _dot(h, wu_ref[:, sl])
            act_ref[rows, sl] = (_silu(gate) * up).astype(BF16)
        ffn = _dot(act_ref[rows, :], wd_ref[...])
        o_ref[rows, :] = xr + mod_ref[:, 5 * D_MODEL:6 * D_MODEL] * ffn


def _out_call(x2d, attn, conv, ret, mod, pw, layer, *, latent, seq, tm):
    rows = x2d.shape[0]
    tiles_per_seq = seq // tm
    if latent:
        mod_map = lambda i: (layer, 1 + i // tiles_per_seq, 0, 0)
    else:
        mod_map = lambda i: (layer, 0, 0, 0)
    row_spec = lambda w: pl.BlockSpec((tm, w), lambda i: (i, 0))
    return pl.pallas_call(
        _out_kernel,
        grid=(rows // tm,),
        in_specs=[
            row_spec(D_MODEL), row_spec(MLA_W), row_spec(CONV_W), row_spec(RET_W),
            pl.BlockSpec((None, None, 1, 6 * D_MODEL), mod_map),
            _layer_spec((1, D_MODEL), layer),
            _layer_spec((D_MODEL, D_MODEL), layer),
            _layer_spec((D_MODEL, FFN_HIDDEN), layer),
            _layer_spec((D_MODEL, FFN_HIDDEN), layer),
            _layer_spec((FFN_HIDDEN, D_MODEL), layer),
        ],
        out_specs=row_spec(D_MODEL),
        out_shape=jax.ShapeDtypeStruct((rows, D_MODEL), F32),
        scratch_shapes=[pltpu.VMEM((tm, FFN_HIDDEN), BF16)],
        compiler_params=pltpu.CompilerParams(
            dimension_semantics=("arbitrary",), vmem_limit_bytes=VMEM_LIMIT),
        name="out_latent" if latent else "out_ctx",
    )(x2d, attn, conv, ret, mod, pw["g2"], pw["wo"], pw["wg"], pw["wu"], pw["wd"])


def _pad_heads(w, heads, width):
    lead = w.shape[:-1]
    w = w.reshape(*lead, heads, width)
    w = jnp.pad(w, [(0, 0)] * len(lead) + [(0, 0), (0, HEAD_PAD - width)])
    return w.reshape(*lead, heads * HEAD_PAD)


def _swap_rope_pairs(w):
    nope = jnp.zeros_like(w[..., :QK_NOPE])
    rope = w[..., QK_NOPE:].reshape(*w.shape[:-1], 2, 2, ROPE_PAIR)
    rope = rope[..., ::-1, :].reshape(*w.shape[:-1], QK_ROPE)
    return jnp.concatenate([nope, rope], axis=-1)


def _w_in_layout_kernel(wt_ref, o_ref):
    k = wt_ref.shape[1]

    def put(dst, rows_t):
        o_ref[:, dst:dst + rows_t.shape[0]] = rows_t.T.astype(BF16)

    for c0 in range(0, C_KPE, LANES):
        put(c0, wt_ref[c0:c0 + LANES, :])
    put(C_KPE, jnp.concatenate([jnp.zeros((QK_NOPE, k), F32), wt_ref[C_KPE:C_KPE + QK_ROPE, :],
                                jnp.zeros((LANES - QK_HEAD, k), F32)], axis=0))
    shift = C_GB - (C_KPE + QK_ROPE)
    for c0 in range(C_GB, IN_COLS_PAD, MXU_N):
        put(c0, wt_ref[c0 - shift:c0 - shift + MXU_N, :])


def _w_in_layout_call(w_in):
    d, k, n = w_in.shape
    return pl.pallas_call(
        _w_in_layout_kernel,
        grid=(d,),
        in_specs=[pl.BlockSpec((None, n, k), lambda l: (l, 0, 0))],
        out_specs=pl.BlockSpec((None, k, IN_COLS_PAD), lambda l: (l, 0, 0)),
        out_shape=jax.ShapeDtypeStruct((d, k, IN_COLS_PAD), BF16),
        compiler_params=pltpu.CompilerParams(
            dimension_semantics=("arbitrary",), vmem_limit_bytes=VMEM_LIMIT),
        name="w_in_layout",
    )(jnp.swapaxes(w_in, 1, 2))


def _prep_weights(w_in, q_norm_g, kv_norm_g, w_q_up, w_kv_up, q_head_norm_g, k_head_norm_g,
                  conv_w, norm1_g, norm2_g, w_o, w_ffn_gate, w_ffn_up, w_ffn_down):
    d = DEPTH
    wq_h = w_q_up.reshape(d, Q_LORA, MLA_HEADS, QK_HEAD)
    wq = jnp.concatenate([_pad_heads(wq_h.reshape(d, Q_LORA, -1), MLA_HEADS, QK_HEAD),
                          _pad_heads(_swap_rope_pairs(wq_h).reshape(d, Q_LORA, -1), MLA_HEADS, QK_HEAD)],
                         axis=-1)
    kvu = w_kv_up.reshape(d, KV_LORA, MLA_HEADS, QK_NOPE + V_HEAD)
    wk = _pad_heads(kvu[..., :QK_NOPE].reshape(d, KV_LORA, -1), MLA_HEADS, QK_NOPE)
    wv = kvu[..., QK_NOPE:].reshape(d, KV_LORA, MLA_W)
    lane_pad = ((0, 0), (0, HEAD_PAD - QK_HEAD))
    qhg = jnp.stack([jnp.pad(q_head_norm_g, lane_pad),
                     jnp.pad(_swap_rope_pairs(q_head_norm_g), lane_pad)], axis=1)
    return {
        "g1": norm1_g.reshape(d, 1, D_MODEL),
        "g2": norm2_g.reshape(d, 1, D_MODEL),
        "w_in": _w_in_layout_call(w_in),
        "qng": q_norm_g.reshape(d, 1, Q_LORA),
        "kvng": kv_norm_g.reshape(d, 1, KV_LORA),
        "wq": wq.astype(BF16),
        "wkv": jnp.concatenate([wk, wv], axis=-1).astype(BF16),
        "qhg": jnp.pad(qhg, ((0, 0), (0, SUBLANES - 2), (0, 0))),
        "khg": jnp.pad(k_head_norm_g, lane_pad).reshape(d, 1, LANES),
        "convw": jnp.pad(conv_w, ((0, 0), (0, SUBLANES - 3), (0, 0))),
        "wo": w_o.astype(BF16),
        "wg": w_ffn_gate.astype(BF16),
        "wu": w_ffn_up.astype(BF16),
        "wd": w_ffn_down.astype(BF16),
    }


def _rope_tables(t):
    half = QK_ROPE // 2
    lane = jnp.arange(LANES)
    rel = lane - QK_NOPE
    in_rope = (rel >= 0) & (rel < QK_ROPE)
    pair_pos = (jnp.clip(rel, 0, QK_ROPE - 1) % ROPE_PAIR).astype(F32)
    freq_lane = jnp.power(ROPE_THETA, -(2.0 * pair_pos) / half)
    first = in_rope & ((rel % half) < ROPE_PAIR)
    second = in_rope & ((rel % half) >= ROPE_PAIR)

    def stack(n_pos):
        ang = jnp.arange(n_pos, dtype=F32)[:, None] * freq_lane[None, :]
        sin = jnp.sin(ang)
        return jnp.stack([jnp.where(in_rope[None, :], jnp.cos(ang), 1.0),
                          jnp.where(first[None, :], -sin, 0.0),
                          jnp.where(second[None, :], sin, 0.0)])

    return stack(t // GRID_W), stack(GRID_W)


def _decay_lanes(ret_decay_fwd, ret_decay_bwd):
    lg = jnp.stack([jax.nn.log_sigmoid(ret_decay_fwd.astype(F32)),
                    jax.nn.log_sigmoid(ret_decay_bwd.astype(F32))], axis=1)
    pad = ((0, 0), (0, SUBLANES - 2), (0, 0))
    return (jnp.pad(jnp.repeat(lg, RET_DV, axis=2), pad),
            jnp.pad(jnp.repeat(lg, RET_CHUNK, axis=2), pad))


TM = 1024
PROJ_SUB = 256
PROJ_CONV_SLOT = 1
TM_OUT = 1024
OUT_SUB = 256
RET_GROUP = 4
RET_GROUP_LATENT = 2
TQ = 2048
TQ_UNIT = 128
CTX_ATTN_LOOKAHEAD = 2


def kernel(x_prompt, x_sample, cache_ckv, cache_kpe, state_ret, c, c_ctx, ada_w, ada_b, norm1_g, norm2_g, w_in, q_norm_g, kv_norm_g, w_q_up, w_kv_up, q_head_norm_g, k_head_norm_g, conv_w, ret_decay_fwd, ret_decay_bwd, w_o, w_ffn_gate, w_ffn_up, w_ffn_down):
    bp, tp, _ = x_prompt.shape
    bs, ts, _ = x_sample.shape

    cond = jnp.concatenate([c_ctx[None, :], c, jnp.zeros((SUBLANES - 1 - bs, D_MODEL), F32)], axis=0)
    mods = _ada_call(cond, ada_w, ada_b).reshape(DEPTH, SUBLANES, 1, 6 * D_MODEL)

    pw = _prep_weights(w_in, q_norm_g, kv_norm_g, w_q_up, w_kv_up, q_head_norm_g, k_head_norm_g,
                       conv_w, norm1_g, norm2_g, w_o, w_ffn_gate, w_ffn_up, w_ffn_down)
    rope_tabs = _rope_tables(ts)
    lgl, lgc = _decay_lanes(ret_decay_fwd, ret_decay_bwd)
    kpe_ctx = jnp.pad(cache_kpe, ((0, 0), (0, 0), (0, 0), (QK_NOPE, LANES - QK_HEAD)))
    kc, vc = _ctx_kv_call(cache_ckv, kpe_ctx, pw["wkv"], pw["khg"])

    xp = x_prompt.reshape(bp * tp, D_MODEL)
    xs = x_sample.reshape(bs * ts, D_MODEL)
    caches, new_state = None, None
    for l in range(DEPTH):
        attn, *caches, conv, rq, rk, rv, rg = _proj_call(
            xp, mods, pw, l, None, caches, latent=False, seq=tp, tm=TM)
        r3 = lambda a: a.reshape(bp, tp, RET_W)
        ret, new_state = _ret_call(lgl, lgc, r3(rq), r3(rk), r3(rv), r3(rg), new_state, l,
                                   latent=False, group=RET_GROUP)
        xp = _out_call(xp, attn, conv, ret.reshape(bp * tp, RET_W), mods, pw, l,
                       latent=False, seq=tp, tm=TM_OUT)

        q, k, v, conv, rq, rk, rv, rg = _proj_call(
            xs, mods, pw, l, rope_tabs, (), latent=True, seq=ts, tm=TM)
        attn = _attn_call(q.reshape(bs, ts, QK_W), k.reshape(bs, ts, QK_W),
                          v.reshape(bs, ts, MLA_W), kc, vc, l, tq=TQ, unit=TQ_UNIT)
        r3 = lambda a: a.reshape(bs, ts, RET_W)
        (ret,) = _ret_call(lgl, lgc, r3(rq), r3(rk), r3(rv), r3(rg), state_ret, l,
                           latent=True, group=RET_GROUP_LATENT)
        xs = _out_call(xs, attn.reshape(bs * ts, MLA_W), conv, ret.reshape(bs * ts, RET_W),
                       mods, pw, l, latent=True, seq=ts, tm=TM_OUT)

    return (xp.reshape(bp, tp, D_MODEL), xs.reshape(bs, ts, D_MODEL),
            caches[0], jnp.swapaxes(caches[1], 2, 3), new_state)
```
